```python
import jax, jax.numpy as jnp
from jax import lax
import numpy as np

D_MODEL = 2048
BATCH = 1
SEQ = 8192
DEPTH = 4

CTX_LEN = 256
GRID_W = 64
D_MIX = D_MODEL
D_FOURIER = D_MIX // 4
N_FGROUPS = 4
FG_DIM = D_FOURIER // N_FGROUPS
D_SSD = D_MIX - D_FOURIER
SSD_HEAD_DIM = 64
SSD_HEADS = D_SSD // SSD_HEAD_DIM
SSD_GROUPS = 4
HEADS_PER_GROUP = SSD_HEADS // SSD_GROUPS
D_STATE = 128
D_CONV = 5
CONV_DIM = D_SSD + 2 * SSD_GROUPS * D_STATE
CHUNK = 128
D_IN_PROJ = D_FOURIER + D_SSD + CONV_DIM + 2 * SSD_HEADS
D_FF = 4 * D_MODEL
EPS = 1e-6

kernel_name = 'hybrid_fnet_ssd_dit_trunk'


def rms_norm(x, g):
    xf = x.astype(jnp.float32)
    y = xf * lax.rsqrt(jnp.mean(xf * xf, axis=-1, keepdims=True) + EPS)
    return (y * g.astype(jnp.float32)).astype(x.dtype)


def modulate(h, shift, scale):
    return h * (1 + scale) + shift


def depthwise_conv(u, w, b):
    y = lax.conv_general_dilated(u, w[:, None, :].astype(u.dtype), window_strides=(1,), padding='SAME',
                                 dimension_numbers=('NWC', 'WIO', 'NWC'), feature_group_count=u.shape[-1])
    return y + b.astype(u.dtype)


def fourier_mix(u, w_f):
    bsz, L, _ = u.shape
    ug = u.astype(jnp.float32).reshape(bsz, L, N_FGROUPS, FG_DIM)
    f = jnp.fft.fft2(ug, axes=(1, 3), norm='ortho').real
    y = jnp.einsum('blgc,gcd->blgd', f, w_f.astype(jnp.float32))
    return y.reshape(bsz, L, D_FOURIER).astype(u.dtype)


def ssd_scan(xh, dt, a, bm, cm, h0):
    bsz, L = xh.shape[:2]
    nc = L // CHUNK
    xc = xh.reshape(bsz, nc, CHUNK, SSD_GROUPS, HEADS_PER_GROUP, SSD_HEAD_DIM)
    dtc = dt.reshape(bsz, nc, CHUNK, SSD_GROUPS, HEADS_PER_GROUP)
    bc = bm.reshape(bsz, nc, CHUNK, SSD_GROUPS, D_STATE)
    cc = cm.reshape(bsz, nc, CHUNK, SSD_GROUPS, D_STATE)
    acum = jnp.cumsum(dtc * a, axis=2)
    a_last = acum[:, :, -1]
    xdt = xc * dtc[..., None]
    seg = acum[:, :, :, None] - acum[:, :, None, :]
    lower = jnp.tril(jnp.ones((CHUNK, CHUNK), dtype=bool))[:, :, None, None]
    decay = jnp.exp(jnp.where(lower, seg, -jnp.inf))
    cb = jnp.einsum('bclgn,bcsgn->bclsg', cc, bc)
    y_diag = jnp.einsum('bclsgh,bcsghp->bclghp', decay * cb[..., None], xdt)
    sdecay = jnp.exp(a_last[:, :, None] - acum)
    states = jnp.einsum('bcsgn,bcsghp->bcghpn', bc, xdt * sdecay[..., None])

    def step(h, inp):
        st, dec = inp
        return dec[..., None, None] * h + st, h

    h_final, h_prev = lax.scan(step, h0, (jnp.moveaxis(states, 1, 0), jnp.moveaxis(jnp.exp(a_last), 1, 0)))
    h_prev = jnp.moveaxis(h_prev, 0, 1)
    y_off = jnp.einsum('bclgn,bcghpn->bclghp', cc, h_prev) * jnp.exp(acum)[..., None]
    y = (y_diag + y_off).reshape(bsz, L, SSD_GROUPS, HEADS_PER_GROUP, SSD_HEAD_DIM)
    return y, h_final


def gated_group_rmsnorm(y, z, g):
    bsz, L, _ = y.shape
    u = (y * jax.nn.silu(z.astype(jnp.float32))).reshape(bsz, L, SSD_GROUPS, D_SSD // SSD_GROUPS)
    u = u * lax.rsqrt(jnp.mean(u * u, axis=-1, keepdims=True) + EPS)
    return u.reshape(bsz, L, D_SSD) * g.astype(jnp.float32)


def ssd_mixer(p_c, p_l, conv_w, conv_b, dt_bias, a_log, d_skip, g_norm, ctx_out):
    f32 = jnp.float32
    out_dtype = p_l.dtype

    def split(p):
        return p[..., :D_SSD], p[..., D_SSD:D_SSD + CONV_DIM], p[..., D_SSD + CONV_DIM:]

    z_c, xbc_c, dtr_c = split(p_c)
    z_l, xbc_l, dtr_l = split(p_l)
    bsz, n_lat = p_l.shape[:2]
    rows = n_lat // GRID_W
    xbc_c = jax.nn.silu(depthwise_conv(xbc_c, conv_w, conv_b))
    xbc_l = jax.nn.silu(depthwise_conv(xbc_l.reshape(bsz * rows, GRID_W, CONV_DIM), conv_w, conv_b))
    xbc_l = xbc_l.reshape(bsz, n_lat, CONV_DIM)
    a = -jnp.exp(a_log.astype(f32)).reshape(2, SSD_GROUPS, HEADS_PER_GROUP)
    dtb = dt_bias.astype(f32).reshape(2, SSD_GROUPS, HEADS_PER_GROUP)

    def heads(xbc, dtr):
        L = xbc.shape[1]
        xbc = xbc.astype(f32)
        xh = xbc[..., :D_SSD].reshape(bsz, L, SSD_GROUPS, HEADS_PER_GROUP, SSD_HEAD_DIM)
        bm = xbc[..., D_SSD:D_SSD + SSD_GROUPS * D_STATE].reshape(bsz, L, SSD_GROUPS, D_STATE)
        cm = xbc[..., D_SSD + SSD_GROUPS * D_STATE:].reshape(bsz, L, SSD_GROUPS, D_STATE)
        dt = jax.nn.softplus(dtr.astype(f32).reshape(bsz, L, 2, SSD_GROUPS, HEADS_PER_GROUP) + dtb)
        return xh, bm, cm, dt[:, :, 0], dt[:, :, 1]

    xh_c, b_c, c_c, dtf_c, dtbk_c = heads(xbc_c, dtr_c)
    xh_l, b_l, c_l, dtf_l, dtbk_l = heads(xbc_l, dtr_l)

    def flip(t):
        return jnp.flip(t, axis=1)

    h0 = jnp.zeros((bsz, SSD_GROUPS, HEADS_PER_GROUP, SSD_HEAD_DIM, D_STATE), f32)
    yf_c, hf_c = ssd_scan(xh_c, dtf_c, a[0], b_c, c_c, h0)
    yb_c, hb_c = ssd_scan(flip(xh_c), flip(dtbk_c), a[1], flip(b_c), flip(c_c), h0)
    yf_l, _ = ssd_scan(xh_l, dtf_l, a[0], b_l, c_l, hf_c)
    yb_l, _ = ssd_scan(flip(xh_l), flip(dtbk_l), a[1], flip(b_l), flip(c_l), hb_c)
    d = d_skip.astype(f32).reshape(SSD_GROUPS, HEADS_PER_GROUP, 1)

    def finish(yf, yb_rev, xh, z):
        L = xh.shape[1]
        y = (yf + flip(yb_rev) + d * xh).reshape(bsz, L, D_SSD)
        return gated_group_rmsnorm(y, z, g_norm).astype(out_dtype)

    y_l = finish(yf_l, yb_l, xh_l, z_l)
    y_c = finish(yf_c, yb_c, xh_c, z_c) if ctx_out else None
    return y_l, y_c


def squared_relu_mlp(h, w1, w2):
    return jnp.square(jax.nn.relu(h @ w1)) @ w2


def setup_inputs(seed: int = 0) -> dict:
    key = jax.random.key(seed)
    ks = jax.random.split(key, 24)
    f32 = jnp.float32
    nrm = lambda k, shape, s: jax.random.normal(k, shape, f32) * s
    u_dt = jax.random.uniform(ks[12], (DEPTH, 2, SSD_HEADS), f32)
    dt0 = jnp.exp(u_dt * (np.log(0.1) - np.log(0.001)) + np.log(0.001)).astype(f32)
    dt_bias = dt0 + jnp.log(-jnp.expm1(-dt0))
    return {
        'x': nrm(ks[0], (BATCH, SEQ, D_MODEL), 1.0),
        'c': nrm(ks[1], (BATCH, D_MODEL), 1.0),
        'ctx': nrm(ks[2], (BATCH, CTX_LEN, D_MODEL), 1.0),
        'c_ctx': nrm(ks[3], (D_MODEL,), 1.0),
        'w_ada': nrm(ks[4], (DEPTH, D_MODEL, 6 * D_MODEL), 0.5 * D_MODEL ** -0.5),
        'b_ada': nrm(ks[5], (DEPTH, 6 * D_MODEL), 0.02),
        'g_mix': 1.0 + nrm(ks[6], (DEPTH, D_MODEL), 0.02),
        'w_in': nrm(ks[7], (DEPTH, D_MODEL, D_IN_PROJ), D_MODEL ** -0.5),
        'conv_w': nrm(ks[8], (DEPTH, D_CONV, CONV_DIM), D_CONV ** -0.5),
        'conv_b': nrm(ks[9], (DEPTH, CONV_DIM), 0.02),
        'dt_bias': dt_bias,
        'a_log': jnp.log(jax.random.uniform(ks[10], (DEPTH, 2, SSD_HEADS), f32, 1.0, 16.0)),
        'd_skip': 1.0 + nrm(ks[11], (DEPTH, SSD_HEADS), 0.1),
        'g_ssd_norm': 1.0 + nrm(ks[13], (DEPTH, D_SSD), 0.02),
        'w_fourier': nrm(ks[14], (DEPTH, N_FGROUPS, FG_DIM, FG_DIM), FG_DIM ** -0.5),
        'w_out': nrm(ks[15], (DEPTH, D_MIX, D_MODEL), D_MIX ** -0.5),
        'g_mlp': 1.0 + nrm(ks[16], (DEPTH, D_MODEL), 0.02),
        'w_mlp1': nrm(ks[17], (DEPTH, D_MODEL, D_FF), D_MODEL ** -0.5),
        'w_mlp2': nrm(ks[18], (DEPTH, D_FF, D_MODEL), D_FF ** -0.5),
        'g_final': 1.0 + nrm(ks[19], (D_MODEL,), 0.02),
    }


def reference(x, c, ctx, c_ctx, w_ada, b_ada, g_mix, w_in, conv_w, conv_b, dt_bias, a_log, d_skip,
              g_ssd_norm, w_fourier, w_out, g_mlp, w_mlp1, w_mlp2, g_final):
    xl = x
    xc = ctx
    sc_lat = jax.nn.silu(c)
    sc_ctx = jax.nn.silu(c_ctx)
    for l in range(DEPTH):
        ctx_out = l < DEPTH - 1
        mod_l = (sc_lat @ w_ada[l] + b_ada[l])[:, None, :]
        mod_c = sc_ctx @ w_ada[l] + b_ada[l]
        sh1_l, sc1_l, gt1_l, sh2_l, sc2_l, gt2_l = jnp.split(mod_l, 6, axis=-1)
        sh1_c, sc1_c, gt1_c, sh2_c, sc2_c, gt2_c = jnp.split(mod_c, 6, axis=-1)
        p_l = modulate(rms_norm(xl, g_mix[l]), sh1_l, sc1_l) @ w_in[l]
        p_c = modulate(rms_norm(xc, g_mix[l]), sh1_c, sc1_c) @ w_in[l]
        f_l = fourier_mix(p_l[..., :D_FOURIER], w_fourier[l])
        y_l, y_c = ssd_mixer(p_c[..., D_FOURIER:], p_l[..., D_FOURIER:], conv_w[l], conv_b[l], dt_bias[l],
                             a_log[l], d_skip[l], g_ssd_norm[l], ctx_out)
        xl = xl + gt1_l * (jnp.concatenate([f_l, y_l], axis=-1) @ w_out[l])
        if ctx_out:
            f_c = fourier_mix(p_c[..., :D_FOURIER], w_fourier[l])
            xc = xc + gt1_c * (jnp.concatenate([f_c, y_c], axis=-1) @ w_out[l])
        xl = xl + gt2_l * squared_relu_mlp(modulate(rms_norm(xl, g_mlp[l]), sh2_l, sc2_l), w_mlp1[l], w_mlp2[l])
        if ctx_out:
            xc = xc + gt2_c * squared_relu_mlp(modulate(rms_norm(xc, g_mlp[l]), sh2_c, sc2_c), w_mlp1[l], w_mlp2[l])
    return rms_norm(xl, g_final)
```

```python
import functools

import numpy as np
import jax
import jax.numpy as jnp
from jax import lax
from jax.experimental import pallas as pl
from jax.experimental.pallas import tpu as pltpu

F32 = jnp.float32
BF16 = jnp.bfloat16
HIGHEST = lax.Precision.HIGHEST

D_MODEL = 2048
SEQ = 8192
CTX_LEN = 256
T_ALL = SEQ + CTX_LEN
DEPTH = 4
GRID_W = 64
D_FOURIER = 512
N_FGROUPS = 4
FG_DIM = 128
D_SSD = 1536
SSD_HEAD_DIM = 64
SSD_HEADS = 24
SSD_GROUPS = 4
HEADS_PER_GROUP = 6
GROUP_DIM = D_SSD // SSD_GROUPS
D_STATE = 128
D_CONV = 5
CONV_DIM = 2560
CHUNK = 128
N_CHUNKS = T_ALL // CHUNK
N_LAT_CHUNKS = SEQ // CHUNK
D_MAIN = D_SSD + CONV_DIM + D_FOURIER
Z_COL = 0
XBC_COL = D_SSD
F_COL = D_SSD + CONV_DIM
DT_PAD = 128
D_FF = 4 * D_MODEL
EPS = 1e-6

LANES = 128
ROW_TILE = 768
N_ROW_BLOCKS = T_ALL // ROW_TILE
COL_TILE = 512
CONV_ROWS = 256
FFT_N1 = 64
FFT_N2 = 128
VMEM_LIMIT = 56 * 1024 * 1024


def _silu(v):
    return v * jax.nn.sigmoid(v)


def _is_ctx_rows(block_idx, rows_per_block):
    row = block_idx * rows_per_block + lax.broadcasted_iota(jnp.int32, (rows_per_block, 1), 0)
    return row >= SEQ


def _pick(is_ctx, ref):
    return jnp.where(is_ctx, ref[1:2, :], ref[0:1, :])


def _norm_modulate(x, g, shift, scale):
    y = x * lax.rsqrt(jnp.mean(x * x, axis=-1, keepdims=True) + EPS) * g
    return y * (1.0 + scale) + shift


def _ada_kernel(c_ref, w_ref, b_ref, o_ref):
    s = _silu(c_ref[...])
    o_ref[0] = jnp.dot(s, w_ref[0], precision=HIGHEST, preferred_element_type=F32) + b_ref[0]


def _ada_call(cc, w_ada, b_ada):
    tn = 1024
    return pl.pallas_call(
        _ada_kernel,
        grid=(DEPTH, 6 * D_MODEL // tn),
        in_specs=[
            pl.BlockSpec((8, D_MODEL), lambda l, j: (0, 0)),
            pl.BlockSpec((1, D_MODEL, tn), lambda l, j: (l, 0, j)),
            pl.BlockSpec((1, 1, tn), lambda l, j: (l, 0, j)),
        ],
        out_specs=pl.BlockSpec((1, 8, tn), lambda l, j: (l, 0, j)),
        out_shape=jax.ShapeDtypeStruct((DEPTH, 8, 6 * D_MODEL), F32),
        compiler_params=pltpu.CompilerParams(
            dimension_semantics=("arbitrary", "arbitrary"), vmem_limit_bytes=VMEM_LIMIT),
        name="ada_mod",
    )(cc, w_ada, b_ada.reshape(DEPTH, 1, 6 * D_MODEL))


def _inproj_kernel(x_ref, g_ref, sh_ref, sc_ref, w_ref, wdt_ref, p_ref, dt_ref, xn_scr):
    i = pl.program_id(0)

    @pl.when(pl.program_id(1) == 0)
    def _():
        is_ctx = _is_ctx_rows(i, ROW_TILE)
        xn = _norm_modulate(x_ref[...], g_ref[...], _pick(is_ctx, sh_ref), _pick(is_ctx, sc_ref))
        xn = xn.astype(BF16)
        xn_scr[...] = xn
        dt_ref[...] = jnp.dot(xn, wdt_ref[...], preferred_element_type=F32)

    p_ref[...] = jnp.dot(xn_scr[...], w_ref[...], preferred_element_type=F32)


def _inproj_call(xs, g, mod, w_main, w_dt):
    return pl.pallas_call(
        _inproj_kernel,
        grid=(N_ROW_BLOCKS, D_MAIN // COL_TILE),
        in_specs=[
            pl.BlockSpec((ROW_TILE, D_MODEL), lambda i, j: (i, 0)),
            pl.BlockSpec((1, D_MODEL), lambda i, j: (0, 0)),
            pl.BlockSpec((8, D_MODEL), lambda i, j: (0, 0)),
            pl.BlockSpec((8, D_MODEL), lambda i, j: (0, 1)),
            pl.BlockSpec((D_MODEL, COL_TILE), lambda i, j: (0, j)),
            pl.BlockSpec((D_MODEL, DT_PAD), lambda i, j: (0, 0)),
        ],
        out_specs=[
            pl.BlockSpec((ROW_TILE, COL_TILE), lambda i, j: (i, j)),
            pl.BlockSpec((ROW_TILE, DT_PAD), lambda i, j: (i, 0)),
        ],
        out_shape=[
            jax.ShapeDtypeStruct((T_ALL, D_MAIN), F32),
            jax.ShapeDtypeStruct((T_ALL, DT_PAD), F32),
        ],
        scratch_shapes=[pltpu.VMEM((ROW_TILE, D_MODEL), BF16)],
        compiler_params=pltpu.CompilerParams(
            dimension_semantics=("arbitrary", "arbitrary"), vmem_limit_bytes=VMEM_LIMIT),
        name="in_proj",
    )(xs, g, mod, mod, w_main, w_dt)


def _dft_tables():
    L, n1, n2 = SEQ, FFT_N1, FFT_N2
    a = np.arange(n1)[:, None, None]
    k2 = np.arange(n2)[None, :, None]
    b = np.arange(n2)[None, None, :]
    ang = 2.0 * np.pi * ((k2 * (a + n1 * b)) % L) / L
    t1 = np.concatenate([np.cos(ang), -np.sin(ang)], axis=1)
    k1 = np.arange(n1)[:, None]
    aa = np.arange(n1)[None, :]
    ang2 = 2.0 * np.pi * ((k1 * aa) % n1) / n1
    c2, s2 = np.cos(ang2), np.sin(ang2)
    f2 = np.block([[c2, s2], [-s2, c2]])
    cc = np.arange(FG_DIM)
    angc = 2.0 * np.pi * ((cc[:, None] * cc[None, :]) % FG_DIM) / FG_DIM
    lc = np.arange(CTX_LEN)
    angl = 2.0 * np.pi * ((lc[:, None] * lc[None, :]) % CTX_LEN) / CTX_LEN
    as32 = lambda v: jnp.asarray(v, dtype=F32)
    return dict(t1=as32(t1), f2=as32(f2), cc=as32(np.cos(angc)), sc=as32(np.sin(angc)),
                cl=as32(np.cos(angl)), sl=as32(np.sin(angl)))


def _fourier_lat_kernel(u_ref, t1_ref, f2_ref, cc_ref, sc_ref, w_ref, o_ref,
                        zr_scr, zi_scr, xr_scr, xi_scr):
    n1, n2 = FFT_N1, FFT_N2
    scale = 1.0 / np.sqrt(float(SEQ) * FG_DIM)
    w = w_ref[0]
    g1 = jnp.dot(cc_ref[...], w, precision=HIGHEST, preferred_element_type=F32) * scale
    g2 = jnp.dot(sc_ref[...], w, precision=HIGHEST, preferred_element_type=F32) * scale

    def stage1(a, carry):
        xa = u_ref[pl.ds(a, n2, stride=n1), :]
        z = jnp.dot(t1_ref[a], xa, precision=HIGHEST, preferred_element_type=F32)
        row0 = pl.multiple_of(a * n2, n2)
        zr_scr[pl.ds(row0, n2), :] = z[:n2]
        zi_scr[pl.ds(row0, n2), :] = z[n2:]
        return carry

    lax.fori_loop(0, n1, stage1, 0)

    def stage2(kp, carry):
        k2 = kp * 2
        cols = []
        for q in range(2):
            zr = zr_scr[pl.ds(k2 + q, n1, stride=n2), :]
            zi = zi_scr[pl.ds(k2 + q, n1, stride=n2), :]
            cols.append(jnp.concatenate([zr, zi], axis=0))
        rhs = jnp.concatenate(cols, axis=1)
        res = jnp.dot(f2_ref[...], rhs, precision=HIGHEST, preferred_element_type=F32)
        for q in range(2):
            blk = res[:, q * FG_DIM:(q + 1) * FG_DIM]
            xr_scr[pl.ds(k2 + q, n1, stride=n2), :] = blk[:n1]
            xi_scr[pl.ds(k2 + q, n1, stride=n2), :] = blk[n1:]
        return carry

    lax.fori_loop(0, n2 // 2, stage2, 0)

    rows = 1024

    def finish(r, carry):
        r0 = pl.multiple_of(r * rows, rows)
        o = (jnp.dot(xr_scr[pl.ds(r0, rows), :], g1, precision=HIGHEST, preferred_element_type=F32)
             + jnp.dot(xi_scr[pl.ds(r0, rows), :], g2, precision=HIGHEST, preferred_element_type=F32))
        o_ref[pl.ds(r0, rows), :] = o.astype(o_ref.dtype)
        return carry

    lax.fori_loop(0, SEQ // rows, finish, 0)


def _fourier_lat_call(p, w_f, tabs):
    col0 = F_COL // FG_DIM
    const2 = lambda shape: pl.BlockSpec(shape, lambda g: (0,) * len(shape))
    return pl.pallas_call(
        _fourier_lat_kernel,
        grid=(N_FGROUPS,),
        in_specs=[
            pl.BlockSpec((SEQ, FG_DIM), lambda g: (0, col0 + g)),
            const2((FFT_N1, 2 * FFT_N2, FFT_N2)),
            const2((2 * FFT_N1, 2 * FFT_N1)),
            const2((FG_DIM, FG_DIM)),
            const2((FG_DIM, FG_DIM)),
            pl.BlockSpec((1, FG_DIM, FG_DIM), lambda g: (g, 0, 0)),
        ],
        out_specs=pl.BlockSpec((SEQ, FG_DIM), lambda g: (0, g)),
        out_shape=jax.ShapeDtypeStruct((T_ALL, D_FOURIER), BF16),
        scratch_shapes=[pltpu.VMEM((SEQ, FG_DIM), F32)] * 4,
        compiler_params=pltpu.CompilerParams(
            dimension_semantics=("arbitrary",), vmem_limit_bytes=VMEM_LIMIT),
        name="fourier_lat",
    )(p, tabs["t1"], tabs["f2"], tabs["cc"], tabs["sc"], w_f)


def _fourier_ctx_kernel(u_ref, cl_ref, sl_ref, cc_ref, sc_ref, w_ref, f_hbm_ref, o_ref):
    del f_hbm_ref
    scale = 1.0 / np.sqrt(float(CTX_LEN) * FG_DIM)
    dot = functools.partial(jnp.dot, precision=HIGHEST, preferred_element_type=F32)
    w = w_ref[0]
    u = u_ref[...]
    a = dot(u, dot(cc_ref[...], w))
    b = dot(u, dot(sc_ref[...], w))
    o_ref[...] = ((dot(cl_ref[...], a) - dot(sl_ref[...], b)) * scale).astype(o_ref.dtype)


def _fourier_ctx_call(p, w_f, tabs, f):
    col0 = F_COL // FG_DIM
    row_blk = SEQ // CTX_LEN
    const2 = lambda shape: pl.BlockSpec(shape, lambda g: (0,) * len(shape))
    return pl.pallas_call(
        _fourier_ctx_kernel,
        grid=(N_FGROUPS,),
        in_specs=[
            pl.BlockSpec((CTX_LEN, FG_DIM), lambda g: (row_blk, col0 + g)),
            const2((CTX_LEN, CTX_LEN)),
            const2((CTX_LEN, CTX_LEN)),
            const2((FG_DIM, FG_DIM)),
            const2((FG_DIM, FG_DIM)),
            pl.BlockSpec((1, FG_DIM, FG_DIM), lambda g: (g, 0, 0)),
            pl.BlockSpec(memory_space=pl.ANY),
        ],
        out_specs=pl.BlockSpec((CTX_LEN, FG_DIM), lambda g: (row_blk, g)),
        out_shape=jax.ShapeDtypeStruct((T_ALL, D_FOURIER), BF16),
        input_output_aliases={6: 0},
        compiler_params=pltpu.CompilerParams(
            dimension_semantics=("arbitrary",), vmem_limit_bytes=VMEM_LIMIT),
        name="fourier_ctx",
    )(p, tabs["cl"], tabs["sl"], tabs["cc"], tabs["sc"], w_f, f)


def _conv_kernel(u_ref, w_ref, b_ref, o_ref):
    i = pl.program_id(0)
    period = jnp.where(i == T_ALL // CONV_ROWS - 1, CONV_ROWS, GRID_W)
    pos = lax.broadcasted_iota(jnp.int32, (CONV_ROWS, 1), 0) & (period - 1)
    u = u_ref[...]
    half = D_CONV // 2
    acc = u * w_ref[half:half + 1, :]
    for k in range(D_CONV):
        off = k - half
        if off == 0:
            continue
        shifted = pltpu.roll(u, (-off) % CONV_ROWS, axis=0)
        valid = jnp.logical_and(pos + off >= 0, pos + off < period)
        acc = acc + jnp.where(valid, shifted, 0.0) * w_ref[k:k + 1, :]
    o_ref[...] = _silu(acc + b_ref[...])


def _conv_call(p, conv_w, conv_b):
    col0 = XBC_COL // COL_TILE
    return pl.pallas_call(
        _conv_kernel,
        grid=(T_ALL // CONV_ROWS, CONV_DIM // COL_TILE),
        in_specs=[
            pl.BlockSpec((CONV_ROWS, COL_TILE), lambda i, j: (i, col0 + j)),
            pl.BlockSpec((8, COL_TILE), lambda i, j: (0, j)),
            pl.BlockSpec((1, COL_TILE), lambda i, j: (0, j)),
        ],
        out_specs=pl.BlockSpec((CONV_ROWS, COL_TILE), lambda i, j: (i, j)),
        out_shape=jax.ShapeDtypeStruct((T_ALL, CONV_DIM), F32),
        compiler_params=pltpu.CompilerParams(
            dimension_semantics=("arbitrary", "arbitrary"), vmem_limit_bytes=VMEM_LIMIT),
        name="conv_silu",
    )(p, conv_w, conv_b)


def _ssd_kernel(direction, *refs):
    forward = direction == 0
    if forward:
        xbc_ref, dtr_ref, dtb_ref, alog_ref, e_ref, tri_ref, o_ref, h_scr = refs
    else:
        (xbc_ref, dtr_ref, dtb_ref, alog_ref, e_ref, tri_ref,
         yf_ref, z_ref, dsk_ref, gn_ref, o_ref, h_scr) = refs
    q = CHUNK
    edge = q - 1 if forward else 0

    @pl.when(pl.program_id(0) == 0)
    def _():
        h_scr[...] = jnp.zeros_like(h_scr)

    v = dtr_ref[...] + dtb_ref[...]
    dt = jnp.maximum(v, 0.0) + jnp.log1p(jnp.exp(-jnp.abs(v)))
    dta = dt * (-jnp.exp(alog_ref[...]))
    cum = jnp.dot(tri_ref[...], dta, precision=HIGHEST, preferred_element_type=F32)
    cum_t = cum.T
    sdec = jnp.exp(cum[edge:edge + 1, :] - cum)
    stack = jnp.concatenate([dt, dt * sdec, jnp.exp(cum)], axis=0)
    ex = jnp.dot(stack, e_ref[...], precision=HIGHEST, preferred_element_type=F32)
    dt_x, ds_x, ec_x = ex[:q], ex[q:2 * q], ex[2 * q:]
    dec_row = ec_x[edge:edge + 1, :]

    x = xbc_ref[:, :D_SSD]
    xdt = (x * dt_x).astype(BF16)
    xs = (x * ds_x).astype(BF16)

    row = lax.broadcasted_iota(jnp.int32, (q, q), 0)
    col = lax.broadcasted_iota(jnp.int32, (q, q), 1)
    keep = (col <= row) if forward else (col >= row)
    low_lanes = col < SSD_HEAD_DIM

    ys = []
    for g in range(SSD_GROUPS):
        gs = slice(g * GROUP_DIM, (g + 1) * GROUP_DIM)
        bm = xbc_ref[:, D_SSD + g * D_STATE:D_SSD + (g + 1) * D_STATE]
        cm = xbc_ref[:, D_SSD + (SSD_GROUPS + g) * D_STATE:D_SSD + (SSD_GROUPS + g + 1) * D_STATE]
        bt = bm.T.astype(BF16)
        cb16 = cm.astype(BF16)
        cb = jnp.dot(cb16, bt, preferred_element_type=F32)
        h_prev = h_scr[g]
        y_off = jnp.dot(cb16, h_prev.astype(BF16), preferred_element_type=F32) * ec_x[:, gs]
        h_scr[g] = h_prev * dec_row[:, gs] + jnp.dot(bt, xs[:, gs], preferred_element_type=F32)
        for j in range(HEADS_PER_GROUP // 2):
            mats = []
            for r in range(2):
                hh = SSD_HEADS * direction + HEADS_PER_GROUP * g + 2 * j + r
                seg = cum[:, hh:hh + 1] - cum_t[hh:hh + 1, :]
                mats.append((jnp.exp(jnp.where(keep, seg, -jnp.inf)) * cb).astype(BF16))
            lhs = jnp.concatenate(mats, axis=1)
            c0 = g * GROUP_DIM + j * 2 * SSD_HEAD_DIM
            xp = xdt[:, c0:c0 + 2 * SSD_HEAD_DIM]
            zero = jnp.zeros_like(xp)
            rhs = jnp.concatenate([jnp.where(low_lanes, xp, zero),
                                   jnp.where(low_lanes, zero, xp)], axis=0)
            y_diag = jnp.dot(lhs, rhs, preferred_element_type=F32)
            ys.append(y_diag + y_off[:, j * 2 * SSD_HEAD_DIM:(j + 1) * 2 * SSD_HEAD_DIM])
    y = jnp.concatenate(ys, axis=1)

    if forward:
        o_ref[...] = y
    else:
        y = yf_ref[...] + y + dsk_ref[...] * x
        u = y * _silu(z_ref[...])
        outs = []
        for g in range(SSD_GROUPS):
            ug = u[:, g * GROUP_DIM:(g + 1) * GROUP_DIM]
            outs.append(ug * lax.rsqrt(jnp.mean(ug * ug, axis=-1, keepdims=True) + EPS))
        o_ref[...] = (jnp.concatenate(outs, axis=1) * gn_ref[...]).astype(o_ref.dtype)


def _fwd_chunk(t):
    return jnp.where(t < 2, N_LAT_CHUNKS + t, t - 2)


def _bwd_chunk(t):
    return jnp.where(t < 2, N_CHUNKS - 1 - t, N_CHUNKS - 1 - t)


def _ssd_consts():
    e = np.zeros((2, DT_PAD, D_SSD), np.float32)
    for d in range(2):
        for h in range(SSD_HEADS):
            e[d, d * SSD_HEADS + h, h * SSD_HEAD_DIM:(h + 1) * SSD_HEAD_DIM] = 1.0
    lower = np.tril(np.ones((CHUNK, CHUNK), np.float32))
    return dict(e=jnp.asarray(e), tri=jnp.asarray(np.stack([lower, lower.T])))


def _ssd_call(direction, xbc, dtr, dtb, alog, consts, extra=None, p=None):
    chunk_of = _fwd_chunk if direction == 0 else _bwd_chunk
    rows = lambda width: pl.BlockSpec((CHUNK, width), lambda t: (chunk_of(t), 0))
    const = lambda shape: pl.BlockSpec(shape, lambda t: (0,) * len(shape))
    in_specs = [rows(CONV_DIM), rows(DT_PAD), const((1, DT_PAD)), const((1, DT_PAD)),
                const((DT_PAD, D_SSD)), const((CHUNK, CHUNK))]
    args = [xbc, dtr, dtb, alog, consts["e"][direction], consts["tri"][direction]]
    if direction == 0:
        out_dtype = F32
    else:
        yf, dsk, gn = extra
        in_specs += [rows(D_SSD), rows(D_SSD), const((1, D_SSD)), const((1, D_SSD))]
        args += [yf, p, dsk, gn]
        out_dtype = BF16
    return pl.pallas_call(
        functools.partial(_ssd_kernel, direction),
        grid=(N_CHUNKS,),
        in_specs=in_specs,
        out_specs=rows(D_SSD),
        out_shape=jax.ShapeDtypeStruct((T_ALL, D_SSD), out_dtype),
        scratch_shapes=[pltpu.VMEM((SSD_GROUPS, D_STATE, GROUP_DIM), F32)],
        compiler_params=pltpu.CompilerParams(
            dimension_semantics=("arbitrary",), vmem_limit_bytes=VMEM_LIMIT),
        name="ssd_fwd" if direction == 0 else "ssd_bwd",
    )(*args)


def _outproj_kernel(x_ref, f_ref, y_ref, wf_ref, wy_ref, gt_ref, o_ref):
    acc = (jnp.dot(f_ref[...], wf_ref[...], preferred_element_type=F32)
           + jnp.dot(y_ref[...], wy_ref[...], preferred_element_type=F32))
    gate = _pick(_is_ctx_rows(pl.program_id(0), ROW_TILE), gt_ref)
    o_ref[...] = x_ref[...] + gate * acc


def _outproj_call(xs, f, y, w_f, w_y, mod):
    gate_blk = 2 * D_MODEL // COL_TILE
    return pl.pallas_call(
        _outproj_kernel,
        grid=(N_ROW_BLOCKS, D_MODEL // COL_TILE),
        in_specs=[
            pl.BlockSpec((ROW_TILE, COL_TILE), lambda i, j: (i, j)),
            pl.BlockSpec((ROW_TILE, D_FOURIER), lambda i, j: (i, 0)),
            pl.BlockSpec((ROW_TILE, D_SSD), lambda i, j: (i, 0)),
            pl.BlockSpec((D_FOURIER, COL_TILE), lambda i, j: (0, j)),
            pl.BlockSpec((D_SSD, COL_TILE), lambda i, j: (0, j)),
            pl.BlockSpec((8, COL_TILE), lambda i, j: (0, gate_blk + j)),
        ],
        out_specs=pl.BlockSpec((ROW_TILE, COL_TILE), lambda i, j: (i, j)),
        out_shape=jax.ShapeDtypeStruct((T_ALL, D_MODEL), F32),
        compiler_params=pltpu.CompilerParams(
            dimension_semantics=("arbitrary", "arbitrary"), vmem_limit_bytes=VMEM_LIMIT),
        name="out_proj",
    )(xs, f, y, w_f, w_y, mod)


def _mlp_kernel(x_ref, g_ref, sh_ref, sc_ref, gt_ref, w1_ref, w2_ref, o_ref, xn_scr):
    i = pl.program_id(0)
    j = pl.program_id(1)

    @pl.when(j == 0)
    def _():
        is_ctx = _is_ctx_rows(i, ROW_TILE)
        xn = _norm_modulate(x_ref[...], g_ref[...], _pick(is_ctx, sh_ref), _pick(is_ctx, sc_ref))
        xn_scr[...] = xn.astype(BF16)
        o_ref[...] = jnp.zeros_like(o_ref)

    h = jnp.dot(xn_scr[...], w1_ref[...], preferred_element_type=F32)
    h = jnp.square(jnp.maximum(h, 0.0)).astype(BF16)
    o_ref[...] += jnp.dot(h, w2_ref[...], preferred_element_type=F32)

    @pl.when(j == pl.num_programs(1) - 1)
    def _():
        gate = _pick(_is_ctx_rows(i, ROW_TILE), gt_ref)
        o_ref[...] = x_ref[...] + gate * o_ref[...]


def _mlp_call(xs, g, mod, w1, w2):
    return pl.pallas_call(
        _mlp_kernel,
        grid=(N_ROW_BLOCKS, D_FF // COL_TILE),
        in_specs=[
            pl.BlockSpec((ROW_TILE, D_MODEL), lambda i, j: (i, 0)),
            pl.BlockSpec((1, D_MODEL), lambda i, j: (0, 0)),
            pl.BlockSpec((8, D_MODEL), lambda i, j: (0, 3)),
            pl.BlockSpec((8, D_MODEL), lambda i, j: (0, 4)),
            pl.BlockSpec((8, D_MODEL), lambda i, j: (0, 5)),
            pl.BlockSpec((D_MODEL, COL_TILE), lambda i, j: (0, j)),
            pl.BlockSpec((COL_TILE, D_MODEL), lambda i, j: (j, 0)),
        ],
        out_specs=pl.BlockSpec((ROW_TILE, D_MODEL), lambda i, j: (i, 0)),
        out_shape=jax.ShapeDtypeStruct((T_ALL, D_MODEL), F32),
        scratch_shapes=[pltpu.VMEM((ROW_TILE, D_MODEL), BF16)],
        compiler_params=pltpu.CompilerParams(
            dimension_semantics=("arbitrary", "arbitrary"), vmem_limit_bytes=VMEM_LIMIT),
        name="mlp",
    )(xs, g, mod, mod, mod, w1, w2)


def _final_norm_kernel(x_ref, g_ref, o_ref):
    x = x_ref[...]
    o_ref[...] = x * lax.rsqrt(jnp.mean(x * x, axis=-1, keepdims=True) + EPS) * g_ref[...]


def _final_norm_call(xs, g):
    rows = 512
    return pl.pallas_call(
        _final_norm_kernel,
        grid=(SEQ // rows,),
        in_specs=[pl.BlockSpec((rows, D_MODEL), lambda i: (i, 0)),
                  pl.BlockSpec((1, D_MODEL), lambda i: (0, 0))],
        out_specs=pl.BlockSpec((rows, D_MODEL), lambda i: (i, 0)),
        out_shape=jax.ShapeDtypeStruct((SEQ, D_MODEL), F32),
        compiler_params=pltpu.CompilerParams(
            dimension_semantics=("arbitrary",), vmem_limit_bytes=VMEM_LIMIT),
        name="final_norm",
    )(xs, g)


def kernel(x, c, ctx, c_ctx, w_ada, b_ada, g_mix, w_in, conv_w, conv_b, dt_bias, a_log, d_skip,
           g_ssd_norm, w_fourier, w_out, g_mlp, w_mlp1, w_mlp2, g_final):
    assert x.shape == (1, SEQ, D_MODEL) and ctx.shape == (1, CTX_LEN, D_MODEL)
    tabs = _dft_tables()
    ssd_consts = _ssd_consts()

    xs = jnp.concatenate([x[0], ctx[0]], axis=0)
    cc = jnp.concatenate([c, c_ctx[None, :], jnp.zeros((6, D_MODEL), F32)], axis=0)
    mods = _ada_call(cc, w_ada, b_ada)

    for l in range(DEPTH):
        mod = mods[l]
        wl = w_in[l]
        w_main = jnp.concatenate(
            [wl[:, D_FOURIER:D_FOURIER + D_SSD + CONV_DIM], wl[:, :D_FOURIER]], axis=1).astype(BF16)
        w_dt = jnp.pad(wl[:, D_FOURIER + D_SSD + CONV_DIM:],
                       ((0, 0), (0, DT_PAD - 2 * SSD_HEADS))).astype(BF16)
        p, dtr = _inproj_call(xs, g_mix[l][None, :], mod, w_main, w_dt)

        f = _fourier_lat_call(p, w_fourier[l], tabs)
        f = _fourier_ctx_call(p, w_fourier[l], tabs, f)

        cw = jnp.pad(conv_w[l], ((0, 8 - D_CONV), (0, 0)))
        xbc = _conv_call(p, cw, conv_b[l][None, :])
        dtb = jnp.pad(dt_bias[l].reshape(1, -1), ((0, 0), (0, DT_PAD - 2 * SSD_HEADS)))
        alog = jnp.pad(a_log[l].reshape(1, -1), ((0, 0), (0, DT_PAD - 2 * SSD_HEADS)))
        dsk = jnp.repeat(d_skip[l], SSD_HEAD_DIM)[None, :]
        yf = _ssd_call(0, xbc, dtr, dtb, alog, ssd_consts)
        y = _ssd_call(1, xbc, dtr, dtb, alog, ssd_consts,
                      extra=(yf, dsk, g_ssd_norm[l][None, :]), p=p)

        wo = w_out[l].astype(BF16)
        xs = _outproj_call(xs, f, y, wo[:D_FOURIER], wo[D_FOURIER:], mod)
        xs = _mlp_call(xs, g_mlp[l][None, :], mod, w_mlp1[l].astype(BF16), w_mlp2[l].astype(BF16))

    return _final_norm_call(xs, g_final[None, :])[None]
```

```python
import functools

import numpy as np
import jax
import jax.numpy as jnp
from jax import lax
from jax.experimental import pallas as pl
from jax.experimental.pallas import tpu as pltpu

F32 = jnp.float32
BF16 = jnp.bfloat16
HIGHEST = lax.Precision.HIGHEST

D_MODEL = 2048
SEQ = 8192
CTX_LEN = 256
T_ALL = SEQ + CTX_LEN
DEPTH = 4
GRID_W = 64
D_FOURIER = 512
N_FGROUPS = 4
FG_DIM = 128
D_SSD = 1536
SSD_HEAD_DIM = 64
SSD_HEADS = 24
SSD_GROUPS = 4
HEADS_PER_GROUP = 6
PAIRS_PER_GROUP = HEADS_PER_GROUP // 2
GROUP_DIM = D_SSD // SSD_GROUPS
D_STATE = 128
D_CONV = 5
CONV_DIM = 2560
CHUNK = 128
N_CHUNKS = T_ALL // CHUNK
N_LAT_CHUNKS = SEQ // CHUNK
D_MAIN = D_FOURIER + D_SSD + CONV_DIM
D_IN_PROJ = D_MAIN + 2 * SSD_HEADS
DT_PAD = 128
D_FF = 4 * D_MODEL
EPS = 1e-6

SUBLANES = 8
LANES = 128
COL_TILE = 512
MLP_ROWS = 768
PROJ_ROWS = 1408
ROW_CHUNK = 128
FFT_N1 = 64
FFT_N2 = 128
VMEM_LIMIT = 56 * 1024 * 1024

N_PROJ_TILES = D_MAIN // COL_TILE
FIRST_XBC_TILE = (D_FOURIER + D_SSD) // COL_TILE
Z_COL = 0
X_COL = D_SSD
BC_COL = 2 * D_SSD
F_COL = D_SSD + CONV_DIM


def _silu(v):
    return v * jax.nn.sigmoid(v)


def _is_ctx_rows(block_idx, rows_per_block):
    row = block_idx * rows_per_block + lax.broadcasted_iota(jnp.int32, (rows_per_block, 1), 0)
    return row >= SEQ


def _pick(is_ctx, ref):
    return jnp.where(is_ctx, ref[1:2, :], ref[0:1, :])


def _norm_modulate_rows(x_ref, g_ref, sh_ref, sc_ref, xn_scr, block_idx, rows_per_block):
    g = g_ref[0]

    def body(c, carry):
        r0 = pl.multiple_of(c * ROW_CHUNK, ROW_CHUNK)
        is_ctx = block_idx * rows_per_block + r0 >= SEQ
        x = x_ref[pl.ds(r0, ROW_CHUNK), :]
        y = x * lax.rsqrt(jnp.mean(x * x, axis=-1, keepdims=True) + EPS) * g
        y = y * (1.0 + _pick(is_ctx, sc_ref)) + _pick(is_ctx, sh_ref)
        xn_scr[pl.ds(r0, ROW_CHUNK), :] = y.astype(BF16)
        return carry

    lax.fori_loop(0, rows_per_block // ROW_CHUNK, body, 0)


def _ada_kernel(c_ref, w_ref, b_ref, o_ref):
    s = _silu(c_ref[...])
    o_ref[0] = jnp.dot(s, w_ref[0], precision=HIGHEST, preferred_element_type=F32) + b_ref[0]


def _ada_call(cc, w_ada, b_ada):
    tn = 1024
    return pl.pallas_call(
        _ada_kernel,
        grid=(DEPTH, 6 * D_MODEL // tn),
        in_specs=[
            pl.BlockSpec((8, D_MODEL), lambda l, j: (0, 0)),
            pl.BlockSpec((1, D_MODEL, tn), lambda l, j: (l, 0, j)),
            pl.BlockSpec((1, 1, tn), lambda l, j: (l, 0, j)),
        ],
        out_specs=pl.BlockSpec((1, 8, tn), lambda l, j: (l, 0, j)),
        out_shape=jax.ShapeDtypeStruct((DEPTH, 8, 6 * D_MODEL), F32),
        compiler_params=pltpu.CompilerParams(
            dimension_semantics=("arbitrary", "arbitrary"), vmem_limit_bytes=VMEM_LIMIT),
        name="ada_mod",
    )(cc, w_ada, b_ada.reshape(DEPTH, 1, 6 * D_MODEL))


def _conv_silu(v, w_ref, b_ref, group_rows):
    rows, cols = v.shape
    tiles = group_rows // SUBLANES
    v4 = v.reshape(rows // group_rows, tiles, SUBLANES, cols)
    sub = lax.broadcasted_iota(jnp.int32, (1, 1, SUBLANES, cols), 2)
    zero_tile = jnp.zeros((rows // group_rows, 1, SUBLANES, cols), F32)
    half = D_CONV // 2
    acc = v4 * w_ref[0, half:half + 1, :]
    for k in range(D_CONV):
        off = k - half
        if off == 0:
            continue
        r = pltpu.roll(v4, (-off) % SUBLANES, axis=2)
        if off < 0:
            nbr = jnp.concatenate([zero_tile, r[:, :-1]], axis=1)
            shifted = jnp.where(sub < -off, nbr, r)
        else:
            nbr = jnp.concatenate([r[:, 1:], zero_tile], axis=1)
            shifted = jnp.where(sub < SUBLANES - off, r, nbr)
        acc = acc + shifted * w_ref[0, k:k + 1, :]
    return _silu(acc + b_ref[0]).reshape(rows, cols)


def _inproj_kernel(x_ref, g_ref, sh_ref, sc_ref, w_ref, wdt_ref, cw_ref, cb_ref, p_ref, dt_ref, xn_scr):
    i = pl.program_id(0)
    j = pl.program_id(1)
    last = pl.num_programs(0) - 1

    @pl.when(j == 0)
    def _():
        _norm_modulate_rows(x_ref, g_ref, sh_ref, sc_ref, xn_scr, i, PROJ_ROWS)
        dt_ref[...] = jnp.dot(xn_scr[...], wdt_ref[...], preferred_element_type=F32)

    acc = jnp.dot(xn_scr[...], w_ref[0].astype(BF16), preferred_element_type=F32)

    @pl.when(j < FIRST_XBC_TILE)
    def _():
        p_ref[...] = acc

    @pl.when(jnp.logical_and(j >= FIRST_XBC_TILE, i < last))
    def _():
        p_ref[...] = _conv_silu(acc, cw_ref, cb_ref, GRID_W)

    @pl.when(jnp.logical_and(j >= FIRST_XBC_TILE, i == last))
    def _():
        n_lat = PROJ_ROWS - CTX_LEN
        p_ref[:n_lat, :] = _conv_silu(acc[:n_lat], cw_ref, cb_ref, GRID_W)
        p_ref[n_lat:, :] = _conv_silu(acc[n_lat:], cw_ref, cb_ref, CTX_LEN)


def _proj_col_block(j):
    return jnp.where(j == 0, N_PROJ_TILES - 1, j - 1)


def _inproj_call(l, xs, g_mix, mod, w_in, w_dt, conv_w, conv_b):
    xbc_tile = lambda j: jnp.maximum(j - FIRST_XBC_TILE, 0)
    return pl.pallas_call(
        _inproj_kernel,
        grid=(T_ALL // PROJ_ROWS, N_PROJ_TILES),
        in_specs=[
            pl.BlockSpec((PROJ_ROWS, D_MODEL), lambda i, j: (i, 0)),
            pl.BlockSpec((1, 1, D_MODEL), lambda i, j: (l, 0, 0)),
            pl.BlockSpec((8, D_MODEL), lambda i, j: (0, 0)),
            pl.BlockSpec((8, D_MODEL), lambda i, j: (0, 1)),
            pl.BlockSpec((1, D_MODEL, COL_TILE), lambda i, j: (l, 0, j)),
            pl.BlockSpec((D_MODEL, DT_PAD), lambda i, j: (0, 0)),
            pl.BlockSpec((1, D_CONV, COL_TILE), lambda i, j: (l, 0, xbc_tile(j))),
            pl.BlockSpec((1, 1, COL_TILE), lambda i, j: (l, 0, xbc_tile(j))),
        ],
        out_specs=[
            pl.BlockSpec((PROJ_ROWS, COL_TILE), lambda i, j: (i, _proj_col_block(j))),
            pl.BlockSpec((PROJ_ROWS, DT_PAD), lambda i, j: (i, 0)),
        ],
        out_shape=[
            jax.ShapeDtypeStruct((T_ALL, D_MAIN), F32),
            jax.ShapeDtypeStruct((T_ALL, DT_PAD), F32),
        ],
        scratch_shapes=[pltpu.VMEM((PROJ_ROWS, D_MODEL), BF16)],
        compiler_params=pltpu.CompilerParams(
            dimension_semantics=("arbitrary", "arbitrary"), vmem_limit_bytes=VMEM_LIMIT),
        name="in_proj",
    )(xs, g_mix, mod, mod, w_in, w_dt, conv_w, conv_b)


def _dft_tables():
    L, n1, n2 = SEQ, FFT_N1, FFT_N2
    a = np.arange(n1)[:, None, None]
    k2 = np.arange(n2)[None, :, None]
    b = np.arange(n2)[None, None, :]
    ang = 2.0 * np.pi * ((k2 * (a + n1 * b)) % L) / L
    t1 = np.concatenate([np.cos(ang), -np.sin(ang)], axis=1)
    k1 = np.arange(n1)[:, None]
    aa = np.arange(n1)[None, :]
    ang2 = 2.0 * np.pi * ((k1 * aa) % n1) / n1
    c2, s2 = np.cos(ang2), np.sin(ang2)
    f2 = np.block([[c2, s2], [-s2, c2]])
    cc = np.arange(FG_DIM)
    angc = 2.0 * np.pi * ((cc[:, None] * cc[None, :]) % FG_DIM) / FG_DIM
    lc = np.arange(CTX_LEN)
    angl = 2.0 * np.pi * ((lc[:, None] * lc[None, :]) % CTX_LEN) / CTX_LEN
    as32 = lambda v: jnp.asarray(v, dtype=F32)
    as16 = lambda v: jnp.asarray(v, dtype=F32).astype(BF16)
    return dict(t1=as16(t1), f2=as16(f2), cc=as32(np.cos(angc)), sc=as32(np.sin(angc)),
                cl=as32(np.cos(angl)), sl=as32(np.sin(angl)))


def _fourier_lat_kernel(u_ref, t1_ref, f2_ref, cc_ref, sc_ref, w_ref, o_ref,
                        zr_scr, zi_scr, xr_scr, xi_scr):
    n1, n2 = FFT_N1, FFT_N2
    scale = 1.0 / np.sqrt(float(SEQ) * FG_DIM)
    w = w_ref[0, 0]
    g1 = jnp.dot(cc_ref[...], w, precision=HIGHEST, preferred_element_type=F32) * scale
    g2 = jnp.dot(sc_ref[...], w, precision=HIGHEST, preferred_element_type=F32) * scale
    gmat = jnp.concatenate([g1, g2], axis=0).astype(BF16)

    def stage1(a, carry):
        xa = u_ref[pl.ds(a, n2, stride=n1), :].astype(BF16)
        z = jnp.dot(t1_ref[a], xa, preferred_element_type=F32)
        row0 = pl.multiple_of(a * n2, n2)
        zr_scr[pl.ds(row0, n2), :] = z[:n2]
        zi_scr[pl.ds(row0, n2), :] = z[n2:]
        return carry

    lax.fori_loop(0, n1, stage1, 0)

    batch = 4

    def stage2(kb, carry):
        k2 = kb * batch
        cols = []
        for q in range(batch):
            zr = zr_scr[pl.ds(k2 + q, n1, stride=n2), :]
            zi = zi_scr[pl.ds(k2 + q, n1, stride=n2), :]
            cols.append(jnp.concatenate([zr, zi], axis=0).astype(BF16))
        rhs = jnp.concatenate(cols, axis=1)
        res = jnp.dot(f2_ref[...], rhs, preferred_element_type=F32)
        for q in range(batch):
            blk = res[:, q * FG_DIM:(q + 1) * FG_DIM]
            xr_scr[pl.ds(k2 + q, n1, stride=n2), :] = blk[:n1]
            xi_scr[pl.ds(k2 + q, n1, stride=n2), :] = blk[n1:]
        return carry

    lax.fori_loop(0, n2 // batch, stage2, 0)

    rows = 1024

    def finish(r, carry):
        r0 = pl.multiple_of(r * rows, rows)
        xri = jnp.concatenate([xr_scr[pl.ds(r0, rows), :], xi_scr[pl.ds(r0, rows), :]], axis=1)
        o = jnp.dot(xri.astype(BF16), gmat, preferred_element_type=F32)
        o_ref[pl.ds(r0, rows), :] = o.astype(o_ref.dtype)
        return carry

    lax.fori_loop(0, SEQ // rows, finish, 0)


def _fourier_lat_call(l, p, w_f, tabs):
    col0 = F_COL // FG_DIM
    const = lambda shape: pl.BlockSpec(shape, lambda g: (0,) * len(shape))
    return pl.pallas_call(
        _fourier_lat_kernel,
        grid=(N_FGROUPS,),
        in_specs=[
            pl.BlockSpec((SEQ, FG_DIM), lambda g: (0, col0 + g)),
            const((FFT_N1, 2 * FFT_N2, FFT_N2)),
            const((2 * FFT_N1, 2 * FFT_N1)),
            const((FG_DIM, FG_DIM)),
            const((FG_DIM, FG_DIM)),
            pl.BlockSpec((1, 1, FG_DIM, FG_DIM), lambda g: (l, g, 0, 0)),
        ],
        out_specs=pl.BlockSpec((SEQ, FG_DIM), lambda g: (0, g)),
        out_shape=jax.ShapeDtypeStruct((T_ALL, D_FOURIER), BF16),
        scratch_shapes=[pltpu.VMEM((SEQ, FG_DIM), F32)] * 4,
        compiler_params=pltpu.CompilerParams(
            dimension_semantics=("arbitrary",), vmem_limit_bytes=VMEM_LIMIT),
        name="fourier_lat",
    )(p, tabs["t1"], tabs["f2"], tabs["cc"], tabs["sc"], w_f)


def _fourier_ctx_kernel(u_ref, cl_ref, sl_ref, cc_ref, sc_ref, w_ref, f_hbm_ref, o_ref):
    del f_hbm_ref
    scale = 1.0 / np.sqrt(float(CTX_LEN) * FG_DIM)
    dot = functools.partial(jnp.dot, precision=HIGHEST, preferred_element_type=F32)
    w = w_ref[0, 0]
    u = u_ref[...]
    a = dot(u, dot(cc_ref[...], w))
    b = dot(u, dot(sc_ref[...], w))
    o_ref[...] = ((dot(cl_ref[...], a) - dot(sl_ref[...], b)) * scale).astype(o_ref.dtype)


def _fourier_ctx_call(l, p, w_f, tabs, f):
    col0 = F_COL // FG_DIM
    row_blk = SEQ // CTX_LEN
    const = lambda shape: pl.BlockSpec(shape, lambda g: (0,) * len(shape))
    return pl.pallas_call(
        _fourier_ctx_kernel,
        grid=(N_FGROUPS,),
        in_specs=[
            pl.BlockSpec((CTX_LEN, FG_DIM), lambda g: (row_blk, col0 + g)),
            const((CTX_LEN, CTX_LEN)),
            const((CTX_LEN, CTX_LEN)),
            const((FG_DIM, FG_DIM)),
            const((FG_DIM, FG_DIM)),
            pl.BlockSpec((1, 1, FG_DIM, FG_DIM), lambda g: (l, g, 0, 0)),
            pl.BlockSpec(memory_space=pl.ANY),
        ],
        out_specs=pl.BlockSpec((CTX_LEN, FG_DIM), lambda g: (row_blk, g)),
        out_shape=jax.ShapeDtypeStruct((T_ALL, D_FOURIER), BF16),
        input_output_aliases={6: 0},
        compiler_params=pltpu.CompilerParams(
            dimension_semantics=("arbitrary",), vmem_limit_bytes=VMEM_LIMIT),
        name="fourier_ctx",
    )(p, tabs["cl"], tabs["sl"], tabs["cc"], tabs["sc"], w_f, f)


def _ssd_kernel(direction, *refs):
    forward = direction == 0
    if forward:
        x_ref, bc_ref, dtr_ref, dtb_ref, alog_ref, tri_ref, o_ref, h_scr = refs
    else:
        (x_ref, bc_ref, dtr_ref, dtb_ref, alog_ref, tri_ref,
         yf_ref, z_ref, dsk_ref, gn_ref, o_ref, h_scr) = refs
    q = CHUNK
    edge = q - 1 if forward else 0

    @pl.when(pl.program_id(0) == 0)
    def _():
        h_scr[...] = jnp.zeros_like(h_scr)

    v = dtr_ref[...] + dtb_ref[0]
    dt = jnp.maximum(v, 0.0) + jnp.log1p(jnp.exp(-jnp.abs(v)))
    dta = dt * (-jnp.exp(alog_ref[0]))
    cum = jnp.dot(tri_ref[...], dta, precision=HIGHEST, preferred_element_type=F32)
    sdec = jnp.exp(cum[edge:edge + 1, :] - cum)
    cum_t = cum.T
    dt_t = dt.T
    w_t = (dt * sdec).T

    x = x_ref[...]
    x16 = x.astype(BF16)

    row = lax.broadcasted_iota(jnp.int32, (q, q), 0)
    col = lax.broadcasted_iota(jnp.int32, (q, q), 1)
    keep = (col <= row) if forward else (col >= row)
    low_lanes = col < SSD_HEAD_DIM

    ys = []
    for g in range(SSD_GROUPS):
        bm = bc_ref[:, g * D_STATE:(g + 1) * D_STATE]
        cm = bc_ref[:, (SSD_GROUPS + g) * D_STATE:(SSD_GROUPS + g + 1) * D_STATE]
        bt = bm.T
        c16 = cm.astype(BF16)
        cb = jnp.dot(c16, bt.astype(BF16), preferred_element_type=F32)
        h_prev = [h_scr[PAIRS_PER_GROUP * g + j] for j in range(PAIRS_PER_GROUP)]
        y_off = jnp.dot(c16, jnp.concatenate(h_prev, axis=1).astype(BF16),
                        preferred_element_type=F32)
        for j in range(PAIRS_PER_GROUP):
            top, bot, ecol = [], [], []
            for r in range(2):
                hh = SSD_HEADS * direction + HEADS_PER_GROUP * g + 2 * j + r
                colb = jnp.broadcast_to(cum[:, hh:hh + 1], (q, q))
                seg = colb - cum_t[hh:hh + 1, :]
                decay = jnp.exp(jnp.where(keep, seg, -jnp.inf))
                top.append((decay * cb * dt_t[hh:hh + 1, :]).astype(BF16))
                bot.append((bt * w_t[hh:hh + 1, :]).astype(BF16))
                ecol.append(jnp.exp(colb))
            lhs = jnp.concatenate([jnp.concatenate(top, axis=1),
                                   jnp.concatenate(bot, axis=1)], axis=0)
            c0 = g * GROUP_DIM + j * LANES
            xp = x16[:, c0:c0 + LANES]
            zero = jnp.zeros_like(xp)
            rhs = jnp.concatenate([jnp.where(low_lanes, xp, zero),
                                   jnp.where(low_lanes, zero, xp)], axis=0)
            res = jnp.dot(lhs, rhs, preferred_element_type=F32)
            escale = jnp.where(low_lanes, ecol[0], ecol[1])
            ys.append(res[:q] + y_off[:, j * LANES:(j + 1) * LANES] * escale)
            h_scr[PAIRS_PER_GROUP * g + j] = h_prev[j] * escale[edge:edge + 1, :] + res[q:]
    y = jnp.concatenate(ys, axis=1)

    if forward:
        o_ref[...] = y
    else:
        y = yf_ref[...] + y + dsk_ref[0] * x
        u = y * _silu(z_ref[...])
        outs = []
        for g in range(SSD_GROUPS):
            ug = u[:, g * GROUP_DIM:(g + 1) * GROUP_DIM]
            outs.append(ug * lax.rsqrt(jnp.mean(ug * ug, axis=-1, keepdims=True) + EPS))
        o_ref[...] = (jnp.concatenate(outs, axis=1) * gn_ref[0]).astype(o_ref.dtype)


def _fwd_chunk(t):
    return jnp.where(t < N_CHUNKS - N_LAT_CHUNKS, N_LAT_CHUNKS + t, t - (N_CHUNKS - N_LAT_CHUNKS))


def _bwd_chunk(t):
    return N_CHUNKS - 1 - t


def _ssd_tri():
    lower = np.tril(np.ones((CHUNK, CHUNK), np.float32))
    return jnp.asarray(np.stack([lower, lower.T]))


def _ssd_call(l, direction, p, dtr, dtb, alog, tri, extra=None):
    chunk_of = _fwd_chunk if direction == 0 else _bwd_chunk
    rows = lambda width, cblk: pl.BlockSpec((CHUNK, width), lambda t: (chunk_of(t), cblk))
    layer = lambda width: pl.BlockSpec((1, 1, width), lambda t: (l, 0, 0))
    in_specs = [rows(D_SSD, X_COL // D_SSD), rows(2 * SSD_GROUPS * D_STATE, BC_COL // (2 * SSD_GROUPS * D_STATE)),
                rows(DT_PAD, 0), layer(DT_PAD), layer(DT_PAD),
                pl.BlockSpec((1, CHUNK, CHUNK), lambda t: (direction, 0, 0))]
    args = [p, p, dtr, dtb, alog, tri]
    if direction == 0:
        out_dtype = F32
    else:
        yf, dsk, gn = extra
        in_specs += [rows(D_SSD, 0), rows(D_SSD, Z_COL // D_SSD), layer(D_SSD), layer(D_SSD)]
        args += [yf, p, dsk, gn]
        out_dtype = BF16

    def body(*refs):
        refs = list(refs)
        refs[5] = refs[5].at[0]
        _ssd_kernel(direction, *refs)

    return pl.pallas_call(
        body,
        grid=(N_CHUNKS,),
        in_specs=in_specs,
        out_specs=rows(D_SSD, 0),
        out_shape=jax.ShapeDtypeStruct((T_ALL, D_SSD), out_dtype),
        scratch_shapes=[pltpu.VMEM((SSD_GROUPS * PAIRS_PER_GROUP, D_STATE, LANES), F32)],
        compiler_params=pltpu.CompilerParams(
            dimension_semantics=("arbitrary",), vmem_limit_bytes=VMEM_LIMIT),
        name="ssd_fwd" if direction == 0 else "ssd_bwd",
    )(*args)


def _outproj_kernel(x_ref, f_ref, y_ref, w0_ref, w1_ref, w2_ref, w3_ref, gt_ref, o_ref):
    kt = D_FOURIER
    acc = jnp.dot(f_ref[...], w0_ref[0].astype(BF16), preferred_element_type=F32)
    for k, w_ref in enumerate((w1_ref, w2_ref, w3_ref)):
        acc += jnp.dot(y_ref[:, k * kt:(k + 1) * kt], w_ref[0].astype(BF16), preferred_element_type=F32)
    gate = _pick(_is_ctx_rows(pl.program_id(0), PROJ_ROWS), gt_ref)
    o_ref[...] = x_ref[...] + gate * acc


def _outproj_call(l, xs, f, y, w_out, mod):
    gate_blk = 2 * D_MODEL // COL_TILE
    kt = D_FOURIER
    w_spec = lambda k: pl.BlockSpec((1, kt, COL_TILE), lambda i, j: (l, k, j))
    return pl.pallas_call(
        _outproj_kernel,
        grid=(T_ALL // PROJ_ROWS, D_MODEL // COL_TILE),
        in_specs=[
            pl.BlockSpec((PROJ_ROWS, COL_TILE), lambda i, j: (i, j)),
            pl.BlockSpec((PROJ_ROWS, D_FOURIER), lambda i, j: (i, 0)),
            pl.BlockSpec((PROJ_ROWS, D_SSD), lambda i, j: (i, 0)),
            w_spec(0), w_spec(1), w_spec(2), w_spec(3),
            pl.BlockSpec((8, COL_TILE), lambda i, j: (0, gate_blk + j)),
        ],
        out_specs=pl.BlockSpec((PROJ_ROWS, COL_TILE), lambda i, j: (i, j)),
        out_shape=jax.ShapeDtypeStruct((T_ALL, D_MODEL), F32),
        compiler_params=pltpu.CompilerParams(
            dimension_semantics=("arbitrary", "arbitrary"), vmem_limit_bytes=VMEM_LIMIT),
        name="out_proj",
    )(xs, f, y, w_out, w_out, w_out, w_out, mod)


def _mlp_kernel(x_ref, g_ref, sh_ref, sc_ref, gt_ref, w1_ref, w2_ref, o_ref, xn_scr):
    i = pl.program_id(0)
    j = pl.program_id(1)

    @pl.when(j == 0)
    def _():
        _norm_modulate_rows(x_ref, g_ref, sh_ref, sc_ref, xn_scr, i, MLP_ROWS)

    h = jnp.dot(xn_scr[...], w1_ref[0].astype(BF16), preferred_element_type=F32)
    h = jnp.square(jnp.maximum(h, 0.0)).astype(BF16)
    part = jnp.dot(h, w2_ref[0].astype(BF16), preferred_element_type=F32)

    @pl.when(j == 0)
    def _():
        o_ref[...] = part

    @pl.when(jnp.logical_and(j > 0, j < pl.num_programs(1) - 1))
    def _():
        o_ref[...] += part

    @pl.when(j == pl.num_programs(1) - 1)
    def _():
        gate = _pick(_is_ctx_rows(i, MLP_ROWS), gt_ref)
        o_ref[...] = x_ref[...] + gate * (o_ref[...] + part)


def _mlp_call(l, xs, g_mlp, mod, w1, w2):
    return pl.pallas_call(
        _mlp_kernel,
        grid=(T_ALL // MLP_ROWS, D_FF // COL_TILE),
        in_specs=[
            pl.BlockSpec((MLP_ROWS, D_MODEL), lambda i, j: (i, 0)),
            pl.BlockSpec((1, 1, D_MODEL), lambda i, j: (l, 0, 0)),
            pl.BlockSpec((8, D_MODEL), lambda i, j: (0, 3)),
            pl.BlockSpec((8, D_MODEL), lambda i, j: (0, 4)),
            pl.BlockSpec((8, D_MODEL), lambda i, j: (0, 5)),
            pl.BlockSpec((1, D_MODEL, COL_TILE), lambda i, j: (l, 0, j)),
            pl.BlockSpec((1, COL_TILE, D_MODEL), lambda i, j: (l, j, 0)),
        ],
        out_specs=pl.BlockSpec((MLP_ROWS, D_MODEL), lambda i, j: (i, 0)),
        out_shape=jax.ShapeDtypeStruct((T_ALL, D_MODEL), F32),
        scratch_shapes=[pltpu.VMEM((MLP_ROWS, D_MODEL), BF16)],
        compiler_params=pltpu.CompilerParams(
            dimension_semantics=("arbitrary", "arbitrary"), vmem_limit_bytes=VMEM_LIMIT),
        name="mlp",
    )(xs, g_mlp, mod, mod, mod, w1, w2)


def _final_norm_kernel(x_ref, g_ref, o_ref):
    x = x_ref[...]
    o_ref[...] = x * lax.rsqrt(jnp.mean(x * x, axis=-1, keepdims=True) + EPS) * g_ref[...]


def _final_norm_call(xs, g):
    rows = 512
    return pl.pallas_call(
        _final_norm_kernel,
        grid=(SEQ // rows,),
        in_specs=[pl.BlockSpec((rows, D_MODEL), lambda i: (i, 0)),
                  pl.BlockSpec((1, D_MODEL), lambda i: (0, 0))],
        out_specs=pl.BlockSpec((rows, D_MODEL), lambda i: (i, 0)),
        out_shape=jax.ShapeDtypeStruct((SEQ, D_MODEL), F32),
        compiler_params=pltpu.CompilerParams(
            dimension_semantics=("arbitrary",), vmem_limit_bytes=VMEM_LIMIT),
        name="final_norm",
    )(xs, g)


def _mixer_layer(l, xs, mod, tabs, tri, w_dt, g_mix, w_in, conv_w, conv_b, dtb, alog, dsk, g_ssd_norm,
                 w_fourier, w_out):
    p, dtr = _inproj_call(l, xs, g_mix, mod, w_in, w_dt, conv_w, conv_b)
    f = _fourier_lat_call(l, p, w_fourier, tabs)
    f = _fourier_ctx_call(l, p, w_fourier, tabs, f)
    yf = _ssd_call(l, 0, p, dtr, dtb, alog, tri)
    y = _ssd_call(l, 1, p, dtr, dtb, alog, tri, extra=(yf, dsk, g_ssd_norm))
    return _outproj_call(l, xs, f, y, w_out, mod)


def kernel(x, c, ctx, c_ctx, w_ada, b_ada, g_mix, w_in, conv_w, conv_b, dt_bias, a_log, d_skip,
           g_ssd_norm, w_fourier, w_out, g_mlp, w_mlp1, w_mlp2, g_final):
    assert x.shape == (1, SEQ, D_MODEL) and ctx.shape == (1, CTX_LEN, D_MODEL)
    tabs = _dft_tables()
    tri = _ssd_tri()

    xs = jnp.concatenate([x[0], ctx[0]], axis=0)
    cc = jnp.concatenate([c, c_ctx[None, :], jnp.zeros((6, D_MODEL), F32)], axis=0)
    mods = _ada_call(cc, w_ada, b_ada)

    pad_heads = ((0, 0), (0, 0), (0, DT_PAD - 2 * SSD_HEADS))
    dtb = jnp.pad(dt_bias.reshape(DEPTH, 1, 2 * SSD_HEADS), pad_heads)
    alog = jnp.pad(a_log.reshape(DEPTH, 1, 2 * SSD_HEADS), pad_heads)
    dsk = jnp.repeat(d_skip, SSD_HEAD_DIM, axis=1).reshape(DEPTH, 1, D_SSD)
    w_dt = jnp.pad(w_in[:, :, D_MAIN:], pad_heads).astype(BF16)
    row3 = lambda a: a.reshape(DEPTH, 1, a.shape[-1])

    for l in range(DEPTH):
        xs = _mixer_layer(l, xs, mods[l], tabs, tri, w_dt[l], row3(g_mix), w_in, conv_w, row3(conv_b),
                          dtb, alog, dsk, row3(g_ssd_norm), w_fourier, w_out)
        xs = _mlp_call(l, xs, row3(g_mlp), mods[l], w_mlp1, w_mlp2)

    return _final_norm_call(xs, g_final[None, :])[None]
```

```python
import functools

import numpy as np
import jax
import jax.numpy as jnp
from jax import lax
from jax.experimental import pallas as pl
from jax.experimental.pallas import tpu as pltpu

F32 = jnp.float32
BF16 = jnp.bfloat16
HIGHEST = lax.Precision.HIGHEST

D_MODEL = 2048
SEQ = 8192
CTX_LEN = 256
T_ALL = SEQ + CTX_LEN
DEPTH = 4
GRID_W = 64
D_FOURIER = 512
N_FGROUPS = 4
FG_DIM = 128
D_SSD = 1536
SSD_HEAD_DIM = 64
SSD_HEADS = 24
SSD_GROUPS = 4
HEADS_PER_GROUP = 6
PAIRS_PER_GROUP = HEADS_PER_GROUP // 2
GROUP_DIM = D_SSD // SSD_GROUPS
D_STATE = 128
D_CONV = 5
CONV_DIM = 2560
CHUNK = 128
N_CHUNKS = T_ALL // CHUNK
N_LAT_CHUNKS = SEQ // CHUNK
D_MAIN = D_FOURIER + D_SSD + CONV_DIM
D_IN_PROJ = D_MAIN + 2 * SSD_HEADS
DT_PAD = 128
D_FF = 4 * D_MODEL
EPS = 1e-6

SUBLANES = 8
LANES = 128
COL_TILE = 512
MLP_ROWS = 768
PROJ_ROWS = 1408
OUT_ROWS = 528
ROW_CHUNK = 128
PREP_CHUNKS = 6
FFT_N1 = 64
FFT_N2 = 128
VMEM_LIMIT = 56 * 1024 * 1024

N_PROJ_TILES = D_MAIN // COL_TILE
N_PROJ_ROW_BLOCKS = T_ALL // PROJ_ROWS
N_PROJ_STEPS = N_PROJ_ROW_BLOCKS * N_PROJ_TILES
FIRST_XBC_TILE = (D_FOURIER + D_SSD) // COL_TILE
Z_COL = 0
X_COL = D_SSD
BC_COL = 2 * D_SSD
F_COL = D_SSD + CONV_DIM


def _silu(v):
    return v * jax.nn.sigmoid(v)


def _is_ctx_rows(block_idx, rows_per_block):
    row = block_idx * rows_per_block + lax.broadcasted_iota(jnp.int32, (rows_per_block, 1), 0)
    return row >= SEQ


def _pick(is_ctx, ref):
    return jnp.where(is_ctx, ref[1:2, :], ref[0:1, :])


def _norm_modulate_rows(x_ref, g_ref, sh_ref, sc_ref, xn_scr, block_idx, rows_per_block):
    g = g_ref[0]

    def body(c, carry):
        r0 = pl.multiple_of(c * ROW_CHUNK, ROW_CHUNK)
        is_ctx = block_idx * rows_per_block + r0 >= SEQ
        x = x_ref[pl.ds(r0, ROW_CHUNK), :]
        y = x * lax.rsqrt(jnp.mean(x * x, axis=-1, keepdims=True) + EPS) * g
        y = y * (1.0 + _pick(is_ctx, sc_ref)) + _pick(is_ctx, sh_ref)
        xn_scr[pl.ds(r0, ROW_CHUNK), :] = y.astype(BF16)
        return carry

    lax.fori_loop(0, rows_per_block // ROW_CHUNK, body, 0)


def _ada_kernel(c_ref, w_ref, b_ref, o_ref):
    s = _silu(c_ref[...])
    o_ref[0] = jnp.dot(s, w_ref[0], precision=HIGHEST, preferred_element_type=F32) + b_ref[0]


def _ada_call(cc, w_ada, b_ada):
    tn = 1024
    return pl.pallas_call(
        _ada_kernel,
        grid=(DEPTH, 6 * D_MODEL // tn),
        in_specs=[
            pl.BlockSpec((8, D_MODEL), lambda l, j: (0, 0)),
            pl.BlockSpec((1, D_MODEL, tn), lambda l, j: (l, 0, j)),
            pl.BlockSpec((1, 1, tn), lambda l, j: (l, 0, j)),
        ],
        out_specs=pl.BlockSpec((1, 8, tn), lambda l, j: (l, 0, j)),
        out_shape=jax.ShapeDtypeStruct((DEPTH, 8, 6 * D_MODEL), F32),
        compiler_params=pltpu.CompilerParams(
            dimension_semantics=("arbitrary", "arbitrary"), vmem_limit_bytes=VMEM_LIMIT),
        name="ada_mod",
    )(cc, w_ada, b_ada.reshape(DEPTH, 1, 6 * D_MODEL))


def _conv_silu(v, w_ref, b_ref, group_rows, col_slice):
    rows, cols = v.shape
    tiles = group_rows // SUBLANES
    v4 = v.reshape(rows // group_rows, tiles, SUBLANES, cols)
    sub = lax.broadcasted_iota(jnp.int32, (1, 1, SUBLANES, cols), 2)
    zero_tile = jnp.zeros((rows // group_rows, 1, SUBLANES, cols), F32)
    half = D_CONV // 2
    w_ref = w_ref.at[:, :, col_slice]
    b_ref = b_ref.at[:, :, col_slice]
    acc = v4 * w_ref[0, half:half + 1, :]
    for k in range(D_CONV):
        off = k - half
        if off == 0:
            continue
        r = pltpu.roll(v4, (-off) % SUBLANES, axis=2)
        if off < 0:
            nbr = jnp.concatenate([zero_tile, r[:, :-1]], axis=1)
            shifted = jnp.where(sub < -off, nbr, r)
        else:
            nbr = jnp.concatenate([r[:, 1:], zero_tile], axis=1)
            shifted = jnp.where(sub < SUBLANES - off, r, nbr)
        acc = acc + shifted * w_ref[0, k:k + 1, :]
    return _silu(acc + b_ref[0]).reshape(rows, cols)


def _inproj_kernel(x_ref, g_ref, sh_ref, sc_ref, w_ref, wdt_ref, cw_ref, cb_ref, p_ref, dt_ref, xn_scr):
    i = pl.program_id(0)
    j = pl.program_id(1)
    last = pl.num_programs(0) - 1

    @pl.when(j == 0)
    def _():
        _norm_modulate_rows(x_ref, g_ref, sh_ref, sc_ref, xn_scr, i, PROJ_ROWS)
        lane = lax.broadcasted_iota(jnp.int32, (1, DT_PAD), 1)
        wdt = jnp.where(lane < 2 * SSD_HEADS, wdt_ref[0], 0.0).astype(BF16)
        dt_ref[...] = jnp.dot(xn_scr[...], wdt, preferred_element_type=F32)

    def matmul():
        return jnp.dot(xn_scr[...], w_ref[0].astype(BF16), preferred_element_type=F32)

    all_cols = slice(0, COL_TILE)

    @pl.when(j < FIRST_XBC_TILE)
    def _():
        p_ref[...] = matmul()

    @pl.when(jnp.logical_and(j >= FIRST_XBC_TILE, i < last))
    def _():
        p_ref[...] = _conv_silu(matmul(), cw_ref, cb_ref, GRID_W, all_cols)

    @pl.when(jnp.logical_and(j >= FIRST_XBC_TILE, i == last))
    def _():
        n_lat = PROJ_ROWS - CTX_LEN
        acc = matmul()
        p_ref[:n_lat, :] = _conv_silu(acc[:n_lat], cw_ref, cb_ref, GRID_W, all_cols)
        p_ref[n_lat:, :] = _conv_silu(acc[n_lat:], cw_ref, cb_ref, CTX_LEN, all_cols)


def _proj_col_block(j):
    return jnp.where(j == 0, N_PROJ_TILES - 1, j - 1)


def _inproj_call(l, xs, g_mix, mod, w_in, conv_w, conv_b):
    xbc_tile = lambda j: jnp.maximum(j - FIRST_XBC_TILE, 0)
    return pl.pallas_call(
        _inproj_kernel,
        grid=(N_PROJ_ROW_BLOCKS, N_PROJ_TILES),
        in_specs=[
            pl.BlockSpec((PROJ_ROWS, D_MODEL), lambda i, j: (i, 0)),
            pl.BlockSpec((1, 1, D_MODEL), lambda i, j: (l, 0, 0)),
            pl.BlockSpec((8, D_MODEL), lambda i, j: (0, 0)),
            pl.BlockSpec((8, D_MODEL), lambda i, j: (0, 1)),
            pl.BlockSpec((1, D_MODEL, COL_TILE), lambda i, j: (l, 0, j)),
            pl.BlockSpec((1, D_MODEL, DT_PAD), lambda i, j: (l, 0, D_MAIN // DT_PAD)),
            pl.BlockSpec((1, D_CONV, COL_TILE), lambda i, j: (l, 0, xbc_tile(j))),
            pl.BlockSpec((1, 1, COL_TILE), lambda i, j: (l, 0, xbc_tile(j))),
        ],
        out_specs=[
            pl.BlockSpec((PROJ_ROWS, COL_TILE), lambda i, j: (i, _proj_col_block(j))),
            pl.BlockSpec((PROJ_ROWS, DT_PAD), lambda i, j: (i, 0)),
        ],
        out_shape=[
            jax.ShapeDtypeStruct((T_ALL, D_MAIN), F32),
            jax.ShapeDtypeStruct((T_ALL, DT_PAD), F32),
        ],
        scratch_shapes=[pltpu.VMEM((PROJ_ROWS, D_MODEL), BF16)],
        compiler_params=pltpu.CompilerParams(
            dimension_semantics=("arbitrary", "arbitrary"), vmem_limit_bytes=VMEM_LIMIT),
        name="in_proj",
    )(xs, g_mix, mod, mod, w_in, w_in, conv_w, conv_b)


def _dft_tables():
    L, n1, n2 = SEQ, FFT_N1, FFT_N2
    a = np.arange(n1)[:, None, None]
    k2 = np.arange(n2)[None, :, None]
    b = np.arange(n2)[None, None, :]
    ang = 2.0 * np.pi * ((k2 * (a + n1 * b)) % L) / L
    t1 = np.concatenate([np.cos(ang), -np.sin(ang)], axis=1)
    k1 = np.arange(n1)[:, None]
    aa = np.arange(n1)[None, :]
    ang2 = 2.0 * np.pi * ((k1 * aa) % n1) / n1
    c2, s2 = np.cos(ang2), np.sin(ang2)
    f2 = np.block([[c2, s2], [-s2, c2]])
    cc = np.arange(FG_DIM)
    angc = 2.0 * np.pi * ((cc[:, None] * cc[None, :]) % FG_DIM) / FG_DIM
    lc = np.arange(CTX_LEN)
    angl = 2.0 * np.pi * ((lc[:, None] * lc[None, :]) % CTX_LEN) / CTX_LEN
    as32 = lambda v: jnp.asarray(v, dtype=F32)
    as16 = lambda v: jnp.asarray(v, dtype=F32).astype(BF16)
    return dict(t1=as16(t1), f2=as16(f2), cc=as32(np.cos(angc)), sc=as32(np.sin(angc)),
                cl=as32(np.cos(angl)), sl=as32(np.sin(angl)))


def _fourier_lat_kernel(u_ref, t1_ref, f2_ref, cc_ref, sc_ref, w_ref, o_ref,
                        zr_scr, zi_scr, xr_scr, xi_scr):
    n1, n2 = FFT_N1, FFT_N2
    scale = 1.0 / np.sqrt(float(SEQ) * FG_DIM)
    w = w_ref[0, 0]
    g1 = jnp.dot(cc_ref[...], w, precision=HIGHEST, preferred_element_type=F32) * scale
    g2 = jnp.dot(sc_ref[...], w, precision=HIGHEST, preferred_element_type=F32) * scale
    gmat = jnp.concatenate([g1, g2], axis=0).astype(BF16)

    def stage1(a, carry):
        xa = u_ref[pl.ds(a, n2, stride=n1), :].astype(BF16)
        z = jnp.dot(t1_ref[a], xa, preferred_element_type=F32)
        row0 = pl.multiple_of(a * n2, n2)
        zr_scr[pl.ds(row0, n2), :] = z[:n2]
        zi_scr[pl.ds(row0, n2), :] = z[n2:]
        return carry

    lax.fori_loop(0, n1, stage1, 0, unroll=8)

    batch = 4

    def stage2(kb, carry):
        k2 = kb * batch
        cols = []
        for q in range(batch):
            zr = zr_scr[pl.ds(k2 + q, n1, stride=n2), :]
            zi = zi_scr[pl.ds(k2 + q, n1, stride=n2), :]
            cols.append(jnp.concatenate([zr, zi], axis=0).astype(BF16))
        rhs = jnp.concatenate(cols, axis=1)
        res = jnp.dot(f2_ref[...], rhs, preferred_element_type=F32)
        for q in range(batch):
            blk = res[:, q * FG_DIM:(q + 1) * FG_DIM]
            xr_scr[pl.ds(k2 + q, n1, stride=n2), :] = blk[:n1]
            xi_scr[pl.ds(k2 + q, n1, stride=n2), :] = blk[n1:]
        return carry

    lax.fori_loop(0, n2 // batch, stage2, 0, unroll=4)

    rows = 1024

    def finish(r, carry):
        r0 = pl.multiple_of(r * rows, rows)
        xri = jnp.concatenate([xr_scr[pl.ds(r0, rows), :], xi_scr[pl.ds(r0, rows), :]], axis=1)
        o = jnp.dot(xri.astype(BF16), gmat, preferred_element_type=F32)
        o_ref[pl.ds(r0, rows), :] = o.astype(o_ref.dtype)
        return carry

    lax.fori_loop(0, SEQ // rows, finish, 0, unroll=2)


def _fourier_lat_call(l, p, w_f, tabs):
    col0 = F_COL // FG_DIM
    const = lambda shape: pl.BlockSpec(shape, lambda g: (0,) * len(shape))
    return pl.pallas_call(
        _fourier_lat_kernel,
        grid=(N_FGROUPS,),
        in_specs=[
            pl.BlockSpec((SEQ, FG_DIM), lambda g: (0, col0 + g)),
            const((FFT_N1, 2 * FFT_N2, FFT_N2)),
            const((2 * FFT_N1, 2 * FFT_N1)),
            const((FG_DIM, FG_DIM)),
            const((FG_DIM, FG_DIM)),
            pl.BlockSpec((1, 1, FG_DIM, FG_DIM), lambda g: (l, g, 0, 0)),
        ],
        out_specs=pl.BlockSpec((SEQ, FG_DIM), lambda g: (0, g)),
        out_shape=jax.ShapeDtypeStruct((T_ALL, D_FOURIER), BF16),
        scratch_shapes=[pltpu.VMEM((SEQ, FG_DIM), F32)] * 4,
        compiler_params=pltpu.CompilerParams(
            dimension_semantics=("arbitrary",), vmem_limit_bytes=VMEM_LIMIT),
        name="fourier_lat",
    )(p, tabs["t1"], tabs["f2"], tabs["cc"], tabs["sc"], w_f)


def _fourier_ctx_kernel(u_ref, cl_ref, sl_ref, cc_ref, sc_ref, w_ref, f_hbm_ref, o_ref):
    del f_hbm_ref
    scale = 1.0 / np.sqrt(float(CTX_LEN) * FG_DIM)
    dot = functools.partial(jnp.dot, precision=HIGHEST, preferred_element_type=F32)
    w = w_ref[0, 0]
    u = u_ref[...]
    a = dot(u, dot(cc_ref[...], w))
    b = dot(u, dot(sc_ref[...], w))
    o_ref[...] = ((dot(cl_ref[...], a) - dot(sl_ref[...], b)) * scale).astype(o_ref.dtype)


def _fourier_ctx_call(l, p, w_f, tabs, f):
    col0 = F_COL // FG_DIM
    row_blk = SEQ // CTX_LEN
    const = lambda shape: pl.BlockSpec(shape, lambda g: (0,) * len(shape))
    return pl.pallas_call(
        _fourier_ctx_kernel,
        grid=(N_FGROUPS,),
        in_specs=[
            pl.BlockSpec((CTX_LEN, FG_DIM), lambda g: (row_blk, col0 + g)),
            const((CTX_LEN, CTX_LEN)),
            const((CTX_LEN, CTX_LEN)),
            const((FG_DIM, FG_DIM)),
            const((FG_DIM, FG_DIM)),
            pl.BlockSpec((1, 1, FG_DIM, FG_DIM), lambda g: (l, g, 0, 0)),
            pl.BlockSpec(memory_space=pl.ANY),
        ],
        out_specs=pl.BlockSpec((CTX_LEN, FG_DIM), lambda g: (row_blk, g)),
        out_shape=jax.ShapeDtypeStruct((T_ALL, D_FOURIER), BF16),
        input_output_aliases={6: 0},
        compiler_params=pltpu.CompilerParams(
            dimension_semantics=("arbitrary",), vmem_limit_bytes=VMEM_LIMIT),
        name="fourier_ctx",
    )(p, tabs["cl"], tabs["sl"], tabs["cc"], tabs["sc"], w_f, f)


def _ssd_prep_kernel(dtr_ref, dtb_ref, alog_ref, tri_ref, cum_ref, tr_ref):
    q = CHUNK
    is_fwd = lax.broadcasted_iota(jnp.int32, (1, DT_PAD), 1) < SSD_HEADS
    neg_a = -jnp.exp(alog_ref[0])
    for c in range(PREP_CHUNKS):
        rs = slice(c * q, (c + 1) * q)
        v = dtr_ref[rs, :] + dtb_ref[0]
        dt = jnp.maximum(v, 0.0) + jnp.log1p(jnp.exp(-jnp.abs(v)))
        dta = dt * neg_a
        run = jnp.dot(tri_ref[0], dta, precision=HIGHEST, preferred_element_type=F32)
        rev = jnp.dot(tri_ref[1], dta, precision=HIGHEST, preferred_element_type=F32)
        cum = jnp.where(is_fwd, run, rev)
        total = jnp.where(is_fwd, run[q - 1:q, :], rev[0:1, :])
        cum_ref[rs, :] = cum
        tr_ref[c, 0] = cum.T
        tr_ref[c, 1] = dt.T
        tr_ref[c, 2] = (dt * jnp.exp(total - cum)).T


def _ssd_prep_call(l, dtr, dtb, alog, tri):
    layer = lambda width: pl.BlockSpec((1, 1, width), lambda t: (l, 0, 0))
    rows = PREP_CHUNKS * CHUNK
    return pl.pallas_call(
        _ssd_prep_kernel,
        grid=(N_CHUNKS // PREP_CHUNKS,),
        in_specs=[pl.BlockSpec((rows, DT_PAD), lambda t: (t, 0)), layer(DT_PAD), layer(DT_PAD),
                  pl.BlockSpec((2, CHUNK, CHUNK), lambda t: (0, 0, 0))],
        out_specs=[pl.BlockSpec((rows, DT_PAD), lambda t: (t, 0)),
                   pl.BlockSpec((PREP_CHUNKS, 3, DT_PAD, CHUNK), lambda t: (t, 0, 0, 0))],
        out_shape=[jax.ShapeDtypeStruct((T_ALL, DT_PAD), F32),
                   jax.ShapeDtypeStruct((N_CHUNKS, 3, DT_PAD, CHUNK), F32)],
        compiler_params=pltpu.CompilerParams(
            dimension_semantics=("arbitrary",), vmem_limit_bytes=VMEM_LIMIT),
        name="ssd_prep",
    )(dtr, dtb, alog, tri)


def _ssd_kernel(direction, *refs):
    forward = direction == 0
    if forward:
        x_ref, bc_ref, cum_ref, tr_ref, o_ref, h_scr = refs
    else:
        x_ref, bc_ref, cum_ref, tr_ref, yf_ref, z_ref, dsk_ref, gn_ref, o_ref, h_scr = refs
    q = CHUNK
    edge = q - 1 if forward else 0

    @pl.when(pl.program_id(0) == 0)
    def _():
        h_scr[...] = jnp.zeros_like(h_scr)

    cum = cum_ref[...]
    cum_t = tr_ref[0, 0]
    dt_t = tr_ref[0, 1]
    w_t = tr_ref[0, 2]

    x = x_ref[...]
    x16 = x.astype(BF16)

    row = lax.broadcasted_iota(jnp.int32, (q, q), 0)
    col = lax.broadcasted_iota(jnp.int32, (q, q), 1)
    keep = (col <= row) if forward else (col >= row)
    low_lanes = col < SSD_HEAD_DIM

    ys = []
    for g in range(SSD_GROUPS):
        bm = bc_ref[:, g * D_STATE:(g + 1) * D_STATE]
        cm = bc_ref[:, (SSD_GROUPS + g) * D_STATE:(SSD_GROUPS + g + 1) * D_STATE]
        bt = bm.T
        c16 = cm.astype(BF16)
        cb = jnp.dot(c16, bt.astype(BF16), preferred_element_type=F32)
        h_prev = [h_scr[PAIRS_PER_GROUP * g + j] for j in range(PAIRS_PER_GROUP)]
        y_off = jnp.dot(c16, jnp.concatenate(h_prev, axis=1).astype(BF16),
                        preferred_element_type=F32)
        for j in range(PAIRS_PER_GROUP):
            top, bot, ecol = [], [], []
            for r in range(2):
                hh = SSD_HEADS * direction + HEADS_PER_GROUP * g + 2 * j + r
                colb = jnp.broadcast_to(cum[:, hh:hh + 1], (q, q))
                seg = colb - cum_t[hh:hh + 1, :]
                decay = jnp.exp(jnp.where(keep, seg, -jnp.inf))
                top.append((decay * cb * dt_t[hh:hh + 1, :]).astype(BF16))
                bot.append((bt * w_t[hh:hh + 1, :]).astype(BF16))
                ecol.append(jnp.exp(colb))
            lhs = jnp.concatenate([jnp.concatenate(top, axis=1),
                                   jnp.concatenate(bot, axis=1)], axis=0)
            c0 = g * GROUP_DIM + j * LANES
            xp = x16[:, c0:c0 + LANES]
            zero = jnp.zeros_like(xp)
            rhs = jnp.concatenate([jnp.where(low_lanes, xp, zero),
                                   jnp.where(low_lanes, zero, xp)], axis=0)
            res = jnp.dot(lhs, rhs, preferred_element_type=F32)
            escale = jnp.where(low_lanes, ecol[0], ecol[1])
            ys.append(res[:q] + y_off[:, j * LANES:(j + 1) * LANES] * escale)
            h_scr[PAIRS_PER_GROUP * g + j] = h_prev[j] * escale[edge:edge + 1, :] + res[q:]
    y = jnp.concatenate(ys, axis=1)

    if forward:
        o_ref[...] = y
    else:
        y = yf_ref[...] + y + dsk_ref[0] * x
        u = y * _silu(z_ref[...])
        outs = []
        for g in range(SSD_GROUPS):
            ug = u[:, g * GROUP_DIM:(g + 1) * GROUP_DIM]
            outs.append(ug * lax.rsqrt(jnp.mean(ug * ug, axis=-1, keepdims=True) + EPS))
        o_ref[...] = (jnp.concatenate(outs, axis=1) * gn_ref[0]).astype(o_ref.dtype)


def _fwd_chunk(t):
    return jnp.where(t < N_CHUNKS - N_LAT_CHUNKS, N_LAT_CHUNKS + t, t - (N_CHUNKS - N_LAT_CHUNKS))


def _bwd_chunk(t):
    return N_CHUNKS - 1 - t


def _ssd_tri():
    lower = np.tril(np.ones((CHUNK, CHUNK), np.float32))
    return jnp.asarray(np.stack([lower, lower.T]))


def _ssd_call(l, direction, p, cum, tr, extra=None):
    chunk_of = _fwd_chunk if direction == 0 else _bwd_chunk
    rows = lambda width, cblk: pl.BlockSpec((CHUNK, width), lambda t: (chunk_of(t), cblk))
    layer = lambda width: pl.BlockSpec((1, 1, width), lambda t: (l, 0, 0))
    bc_width = 2 * SSD_GROUPS * D_STATE
    in_specs = [rows(D_SSD, X_COL // D_SSD), rows(bc_width, BC_COL // bc_width), rows(DT_PAD, 0),
                pl.BlockSpec((1, 3, DT_PAD, CHUNK), lambda t: (chunk_of(t), 0, 0, 0))]
    args = [p, p, cum, tr]
    if direction == 0:
        out_dtype = F32
    else:
        yf, dsk, gn = extra
        in_specs += [rows(D_SSD, 0), rows(D_SSD, Z_COL // D_SSD), layer(D_SSD), layer(D_SSD)]
        args += [yf, p, dsk, gn]
        out_dtype = BF16
    return pl.pallas_call(
        functools.partial(_ssd_kernel, direction),
        grid=(N_CHUNKS,),
        in_specs=in_specs,
        out_specs=rows(D_SSD, 0),
        out_shape=jax.ShapeDtypeStruct((T_ALL, D_SSD), out_dtype),
        scratch_shapes=[pltpu.VMEM((SSD_GROUPS * PAIRS_PER_GROUP, D_STATE, LANES), F32)],
        compiler_params=pltpu.CompilerParams(
            dimension_semantics=("arbitrary",), vmem_limit_bytes=VMEM_LIMIT),
        name="ssd_fwd" if direction == 0 else "ssd_bwd",
    )(*args)


def _outproj_kernel(x_ref, f_ref, y_ref, w_ref, gt_ref, o_ref, w16_scr):
    i = pl.program_id(0)

    @pl.when(i == 0)
    def _():
        rows = 256

        def body(c, carry):
            rs = pl.ds(pl.multiple_of(c * rows, rows), rows)
            w16_scr[rs, :] = w_ref[0, rs, :].astype(BF16)
            return carry

        lax.fori_loop(0, D_MODEL // rows, body, 0)

    acc = (jnp.dot(f_ref[...], w16_scr[:D_FOURIER, :], preferred_element_type=F32)
           + jnp.dot(y_ref[...], w16_scr[D_FOURIER:, :], preferred_element_type=F32))
    gate = _pick(_is_ctx_rows(i, OUT_ROWS), gt_ref)
    o_ref[...] = x_ref[...] + gate * acc


def _outproj_call(l, xs, f, y, w_out, mod):
    rows = lambda width: pl.BlockSpec((OUT_ROWS, width), lambda i: (i, 0))
    return pl.pallas_call(
        _outproj_kernel,
        grid=(T_ALL // OUT_ROWS,),
        in_specs=[
            rows(D_MODEL), rows(D_FOURIER), rows(D_SSD),
            pl.BlockSpec((1, D_MODEL, D_MODEL), lambda i: (l, 0, 0), pipeline_mode=pl.Buffered(1)),
            pl.BlockSpec((8, D_MODEL), lambda i: (0, 2)),
        ],
        out_specs=rows(D_MODEL),
        out_shape=jax.ShapeDtypeStruct((T_ALL, D_MODEL), F32),
        scratch_shapes=[pltpu.VMEM((D_MODEL, D_MODEL), BF16)],
        compiler_params=pltpu.CompilerParams(
            dimension_semantics=("arbitrary",), vmem_limit_bytes=VMEM_LIMIT),
        name="out_proj",
    )(xs, f, y, w_out, mod)


def _mlp_kernel(x_ref, g_ref, sh_ref, sc_ref, gt_ref, w1_ref, w2_ref, o_ref, xn_scr):
    i = pl.program_id(0)
    j = pl.program_id(1)

    @pl.when(j == 0)
    def _():
        _norm_modulate_rows(x_ref, g_ref, sh_ref, sc_ref, xn_scr, i, MLP_ROWS)
        o_ref[...] = jnp.zeros_like(o_ref)

    h = jnp.dot(xn_scr[...], w1_ref[0], preferred_element_type=F32)
    h = jnp.square(jnp.maximum(h, 0.0)).astype(BF16)
    o_ref[...] += jnp.dot(h, w2_ref[0], preferred_element_type=F32)

    @pl.when(j == pl.num_programs(1) - 1)
    def _():
        gate = _pick(_is_ctx_rows(i, MLP_ROWS), gt_ref)
        o_ref[...] = x_ref[...] + gate * o_ref[...]


def _mlp_call(l, xs, g_mlp, mod, w1, w2):
    return pl.pallas_call(
        _mlp_kernel,
        grid=(T_ALL // MLP_ROWS, D_FF // COL_TILE),
        in_specs=[
            pl.BlockSpec((MLP_ROWS, D_MODEL), lambda i, j: (i, 0)),
            pl.BlockSpec((1, 1, D_MODEL), lambda i, j: (l, 0, 0)),
            pl.BlockSpec((8, D_MODEL), lambda i, j: (0, 3)),
            pl.BlockSpec((8, D_MODEL), lambda i, j: (0, 4)),
            pl.BlockSpec((8, D_MODEL), lambda i, j: (0, 5)),
            pl.BlockSpec((1, D_MODEL, COL_TILE), lambda i, j: (l, 0, j)),
            pl.BlockSpec((1, COL_TILE, D_MODEL), lambda i, j: (l, j, 0)),
        ],
        out_specs=pl.BlockSpec((MLP_ROWS, D_MODEL), lambda i, j: (i, 0)),
        out_shape=jax.ShapeDtypeStruct((T_ALL, D_MODEL), F32),
        scratch_shapes=[pltpu.VMEM((MLP_ROWS, D_MODEL), BF16)],
        compiler_params=pltpu.CompilerParams(
            dimension_semantics=("arbitrary", "arbitrary"), vmem_limit_bytes=VMEM_LIMIT),
        name="mlp",
    )(xs, g_mlp, mod, mod, mod, w1, w2)


def _final_norm_kernel(x_ref, g_ref, o_ref):
    x = x_ref[...]
    o_ref[...] = x * lax.rsqrt(jnp.mean(x * x, axis=-1, keepdims=True) + EPS) * g_ref[...]


def _final_norm_call(xs, g):
    rows = 512
    return pl.pallas_call(
        _final_norm_kernel,
        grid=(SEQ // rows,),
        in_specs=[pl.BlockSpec((rows, D_MODEL), lambda i: (i, 0)),
                  pl.BlockSpec((1, D_MODEL), lambda i: (0, 0))],
        out_specs=pl.BlockSpec((rows, D_MODEL), lambda i: (i, 0)),
        out_shape=jax.ShapeDtypeStruct((SEQ, D_MODEL), F32),
        compiler_params=pltpu.CompilerParams(
            dimension_semantics=("arbitrary",), vmem_limit_bytes=VMEM_LIMIT),
        name="final_norm",
    )(xs, g)


def _mixer_layer(l, xs, mod, tabs, tri, g_mix, w_in, conv_w, conv_b, dtb, alog, dsk, g_ssd_norm,
                 w_fourier, w_out):
    p, dtr = _inproj_call(l, xs, g_mix, mod, w_in, conv_w, conv_b)
    f = _fourier_lat_call(l, p, w_fourier, tabs)
    f = _fourier_ctx_call(l, p, w_fourier, tabs, f)
    cum, tr = _ssd_prep_call(l, dtr, dtb, alog, tri)
    yf = _ssd_call(l, 0, p, cum, tr)
    y = _ssd_call(l, 1, p, cum, tr, extra=(yf, dsk, g_ssd_norm))
    return _outproj_call(l, xs, f, y, w_out, mod)


def kernel(x, c, ctx, c_ctx, w_ada, b_ada, g_mix, w_in, conv_w, conv_b, dt_bias, a_log, d_skip,
           g_ssd_norm, w_fourier, w_out, g_mlp, w_mlp1, w_mlp2, g_final):
    assert x.shape == (1, SEQ, D_MODEL) and ctx.shape == (1, CTX_LEN, D_MODEL)
    tabs = _dft_tables()
    tri = _ssd_tri()

    xs = jnp.concatenate([x[0], ctx[0]], axis=0)
    cc = jnp.concatenate([c, c_ctx[None, :], jnp.zeros((6, D_MODEL), F32)], axis=0)
    mods = _ada_call(cc, w_ada, b_ada)

    pad_heads = ((0, 0), (0, 0), (0, DT_PAD - 2 * SSD_HEADS))
    dtb = jnp.pad(dt_bias.reshape(DEPTH, 1, 2 * SSD_HEADS), pad_heads)
    alog = jnp.pad(a_log.reshape(DEPTH, 1, 2 * SSD_HEADS), pad_heads)
    dsk = jnp.repeat(d_skip, SSD_HEAD_DIM, axis=1).reshape(DEPTH, 1, D_SSD)
    row3 = lambda a: a.reshape(DEPTH, 1, a.shape[-1])
    w1 = w_mlp1.astype(BF16)
    w2 = w_mlp2.astype(BF16)

    for l in range(DEPTH):
        xs = _mixer_layer(l, xs, mods[l], tabs, tri, row3(g_mix), w_in, conv_w, row3(conv_b),
                          dtb, alog, dsk, row3(g_ssd_norm), w_fourier, w_out)
        xs = _mlp_call(l, xs, row3(g_mlp), mods[l], w1, w2)

    return _final_norm_call(xs, g_final[None, :])[None]
```

```python
import functools

import numpy as np
import jax
import jax.numpy as jnp
from jax import lax
from jax.experimental import pallas as pl
from jax.experimental.pallas import tpu as pltpu

F32 = jnp.float32
BF16 = jnp.bfloat16
HIGHEST = lax.Precision.HIGHEST

D_MODEL = 2048
SEQ = 8192
CTX_LEN = 256
T_ALL = SEQ + CTX_LEN
DEPTH = 4
GRID_W = 64
D_FOURIER = 512
N_FGROUPS = 4
FG_DIM = 128
D_SSD = 1536
SSD_HEAD_DIM = 64
SSD_HEADS = 24
SSD_GROUPS = 4
HEADS_PER_GROUP = 6
PAIRS_PER_GROUP = HEADS_PER_GROUP // 2
GROUP_DIM = D_SSD // SSD_GROUPS
D_STATE = 128
D_CONV = 5
CONV_DIM = 2560
CHUNK = 128
N_CHUNKS = T_ALL // CHUNK
N_LAT_CHUNKS = SEQ // CHUNK
D_MAIN = D_FOURIER + D_SSD + CONV_DIM
D_IN_PROJ = D_MAIN + 2 * SSD_HEADS
DT_PAD = 128
D_FF = 4 * D_MODEL
EPS = 1e-6
LOG2_E = float(np.log2(np.e))

SUBLANES = 8
LANES = 128
COL_TILE = 512
MLP_ROWS = 768
PROJ_ROWS = 1408
OUT_ROWS = 528
ADA_ROWS = 16
ROW_CHUNK = 128
PREP_CHUNKS = 6
FFT_N1 = 64
FFT_N2 = 128
VMEM_LIMIT = 56 * 1024 * 1024

N_PROJ_TILES = D_MAIN // COL_TILE
N_PROJ_ROW_BLOCKS = T_ALL // PROJ_ROWS
N_PROJ_STEPS = N_PROJ_ROW_BLOCKS * N_PROJ_TILES
FIRST_XBC_TILE = (D_FOURIER + D_SSD) // COL_TILE
Z_COL = 0
X_COL = D_SSD
BC_COL = 2 * D_SSD
F_COL = D_SSD + CONV_DIM


def _silu(v):
    return v * jax.nn.sigmoid(v)


def _is_ctx_rows(block_idx, rows_per_block):
    row = block_idx * rows_per_block + lax.broadcasted_iota(jnp.int32, (rows_per_block, 1), 0)
    return row >= SEQ


def _pick(is_ctx, ref):
    return jnp.where(is_ctx, ref[1:2, :], ref[0:1, :])


def _norm_modulate_rows(x_ref, g_ref, sh_ref, sc_ref, xn_scr, block_idx, rows_per_block):
    g = g_ref[0]

    def body(c, carry):
        r0 = pl.multiple_of(c * ROW_CHUNK, ROW_CHUNK)
        is_ctx = block_idx * rows_per_block + r0 >= SEQ
        x = x_ref[pl.ds(r0, ROW_CHUNK), :]
        y = x * lax.rsqrt(jnp.mean(x * x, axis=-1, keepdims=True) + EPS) * g
        y = y * (1.0 + _pick(is_ctx, sc_ref)) + _pick(is_ctx, sh_ref)
        xn_scr[pl.ds(r0, ROW_CHUNK), :] = y.astype(BF16)
        return carry

    lax.fori_loop(0, rows_per_block // ROW_CHUNK, body, 0)


def _ada_kernel(c_ref, w_ref, b_ref, o_ref):
    s = _silu(c_ref[...]).astype(BF16)
    o_ref[0] = jnp.dot(s, w_ref[0].astype(BF16), preferred_element_type=F32) + b_ref[0]


def _ada_call(cc, w_ada, b_ada):
    tn = 1024
    return pl.pallas_call(
        _ada_kernel,
        grid=(DEPTH, 6 * D_MODEL // tn),
        in_specs=[
            pl.BlockSpec((ADA_ROWS, D_MODEL), lambda l, j: (0, 0)),
            pl.BlockSpec((1, D_MODEL, tn), lambda l, j: (l, 0, j)),
            pl.BlockSpec((1, 1, tn), lambda l, j: (l, 0, j)),
        ],
        out_specs=pl.BlockSpec((1, ADA_ROWS, tn), lambda l, j: (l, 0, j)),
        out_shape=jax.ShapeDtypeStruct((DEPTH, ADA_ROWS, 6 * D_MODEL), F32),
        compiler_params=pltpu.CompilerParams(
            dimension_semantics=("arbitrary", "arbitrary"), vmem_limit_bytes=VMEM_LIMIT),
        name="ada_mod",
    )(cc, w_ada, b_ada.reshape(DEPTH, 1, 6 * D_MODEL))


def _conv_silu(v, w_ref, b_ref, group_rows, col_slice):
    rows, cols = v.shape
    tiles = group_rows // SUBLANES
    v4 = v.reshape(rows // group_rows, tiles, SUBLANES, cols)
    sub = lax.broadcasted_iota(jnp.int32, (1, 1, SUBLANES, cols), 2)
    zero_tile = jnp.zeros((rows // group_rows, 1, SUBLANES, cols), F32)
    half = D_CONV // 2
    w_ref = w_ref.at[:, :, col_slice]
    b_ref = b_ref.at[:, :, col_slice]
    acc = v4 * w_ref[0, half:half + 1, :]
    for k in range(D_CONV):
        off = k - half
        if off == 0:
            continue
        r = pltpu.roll(v4, (-off) % SUBLANES, axis=2)
        if off < 0:
            nbr = jnp.concatenate([zero_tile, r[:, :-1]], axis=1)
            shifted = jnp.where(sub < -off, nbr, r)
        else:
            nbr = jnp.concatenate([r[:, 1:], zero_tile], axis=1)
            shifted = jnp.where(sub < SUBLANES - off, r, nbr)
        acc = acc + shifted * w_ref[0, k:k + 1, :]
    return _silu(acc + b_ref[0]).reshape(rows, cols)


def _dot_nt(a, b_t):
    return lax.dot_general(a, b_t, (((1,), (1,)), ((), ())), preferred_element_type=F32)


def _inproj_kernel(x_ref, g_ref, sh_ref, sc_ref, w_ref, wdt_ref, cw_ref, cb_ref, p_ref, dt_ref, xn_scr):
    i = pl.program_id(0)
    j = pl.program_id(1)
    last = pl.num_programs(0) - 1

    @pl.when(j == 0)
    def _():
        _norm_modulate_rows(x_ref, g_ref, sh_ref, sc_ref, xn_scr, i, PROJ_ROWS)
        row = lax.broadcasted_iota(jnp.int32, (DT_PAD, 1), 0)
        wdt = jnp.where(row < 2 * SSD_HEADS, wdt_ref[0], 0.0).astype(BF16)
        dt_ref[...] = _dot_nt(xn_scr[...], wdt)

    def matmul():
        return _dot_nt(xn_scr[...], w_ref[0].astype(BF16))

    all_cols = slice(0, COL_TILE)

    @pl.when(j < FIRST_XBC_TILE)
    def _():
        p_ref[...] = matmul()

    @pl.when(jnp.logical_and(j >= FIRST_XBC_TILE, i < last))
    def _():
        w16 = w_ref[0].T.astype(BF16)
        bounds = tuple(range(0, PROJ_ROWS + 1, 2 * GRID_W))
        for r0, r1 in zip(bounds[:-1], bounds[1:]):
            acc = jnp.dot(xn_scr[r0:r1, :], w16, preferred_element_type=F32)
            p_ref[r0:r1, :] = _conv_silu(acc, cw_ref, cb_ref, GRID_W, all_cols)

    @pl.when(jnp.logical_and(j >= FIRST_XBC_TILE, i == last))
    def _():
        n_lat = PROJ_ROWS - CTX_LEN
        acc = matmul()
        p_ref[:n_lat, :] = _conv_silu(acc[:n_lat], cw_ref, cb_ref, GRID_W, all_cols)
        p_ref[n_lat:, :] = _conv_silu(acc[n_lat:], cw_ref, cb_ref, CTX_LEN, all_cols)


def _proj_col_block(j):
    return jnp.where(j == 0, N_PROJ_TILES - 1, j - 1)


def _inproj_call(l, xs, g_mix, mod, w_in, conv_w, conv_b):
    xbc_tile = lambda j: jnp.maximum(j - FIRST_XBC_TILE, 0)
    return pl.pallas_call(
        _inproj_kernel,
        grid=(N_PROJ_ROW_BLOCKS, N_PROJ_TILES),
        in_specs=[
            pl.BlockSpec((PROJ_ROWS, D_MODEL), lambda i, j: (i, 0)),
            pl.BlockSpec((1, 1, D_MODEL), lambda i, j: (l, 0, 0)),
            pl.BlockSpec((8, D_MODEL), lambda i, j: (0, 0)),
            pl.BlockSpec((8, D_MODEL), lambda i, j: (0, 1)),
            pl.BlockSpec((1, COL_TILE, D_MODEL), lambda i, j: (l, j, 0)),
            pl.BlockSpec((1, DT_PAD, D_MODEL), lambda i, j: (l, D_MAIN // DT_PAD, 0)),
            pl.BlockSpec((1, D_CONV, COL_TILE), lambda i, j: (l, 0, xbc_tile(j))),
            pl.BlockSpec((1, 1, COL_TILE), lambda i, j: (l, 0, xbc_tile(j))),
        ],
        out_specs=[
            pl.BlockSpec((PROJ_ROWS, COL_TILE), lambda i, j: (i, _proj_col_block(j))),
            pl.BlockSpec((PROJ_ROWS, DT_PAD), lambda i, j: (i, 0)),
        ],
        out_shape=[
            jax.ShapeDtypeStruct((T_ALL, D_MAIN), F32),
            jax.ShapeDtypeStruct((T_ALL, DT_PAD), F32),
        ],
        scratch_shapes=[pltpu.VMEM((PROJ_ROWS, D_MODEL), BF16)],
        compiler_params=pltpu.CompilerParams(
            dimension_semantics=("arbitrary", "arbitrary"), vmem_limit_bytes=VMEM_LIMIT),
        name="in_proj",
    )(xs, g_mix, mod, mod, w_in, w_in, conv_w, conv_b)


def _dft_tables():
    L, n1, n2 = SEQ, FFT_N1, FFT_N2
    a = np.arange(n1)[:, None, None]
    k2 = np.arange(n2)[None, :, None]
    b = np.arange(n2)[None, None, :]
    ang = 2.0 * np.pi * ((k2 * (a + n1 * b)) % L) / L
    t1 = np.concatenate([np.cos(ang), -np.sin(ang)], axis=1)
    k1 = np.arange(n1)[:, None]
    aa = np.arange(n1)[None, :]
    ang2 = 2.0 * np.pi * ((k1 * aa) % n1) / n1
    c2, s2 = np.cos(ang2), np.sin(ang2)
    f2 = np.block([[c2, s2], [-s2, c2]])
    cc = np.arange(FG_DIM)
    angc = 2.0 * np.pi * ((cc[:, None] * cc[None, :]) % FG_DIM) / FG_DIM
    lc = np.arange(CTX_LEN)
    angl = 2.0 * np.pi * ((lc[:, None] * lc[None, :]) % CTX_LEN) / CTX_LEN
    as32 = lambda v: jnp.asarray(v, dtype=F32)
    as16 = lambda v: jnp.asarray(v, dtype=F32).astype(BF16)
    return dict(t1=as16(t1), f2=as16(f2), cc=as32(np.cos(angc)), sc=as32(np.sin(angc)),
                cl=as32(np.cos(angl)), sl=as32(np.sin(angl)))


def _fourier_lat_kernel(u_ref, t1_ref, f2_ref, cc_ref, sc_ref, w_ref, o_ref,
                        zr_scr, zi_scr, xr_scr, xi_scr):
    n1, n2 = FFT_N1, FFT_N2
    scale = 1.0 / np.sqrt(float(SEQ) * FG_DIM)
    w = w_ref[0, 0]
    g1 = jnp.dot(cc_ref[...], w, precision=HIGHEST, preferred_element_type=F32) * scale
    g2 = jnp.dot(sc_ref[...], w, precision=HIGHEST, preferred_element_type=F32) * scale
    gmat = jnp.concatenate([g1, g2], axis=0).astype(BF16)

    def stage1(a, carry):
        xa = u_ref[pl.ds(a, n2, stride=n1), :].astype(BF16)
        z = jnp.dot(t1_ref[a], xa, preferred_element_type=F32)
        row0 = pl.multiple_of(a * n2, n2)
        zr_scr[pl.ds(row0, n2), :] = z[:n2]
        zi_scr[pl.ds(row0, n2), :] = z[n2:]
        return carry

    lax.fori_loop(0, n1, stage1, 0, unroll=8)

    batch = 4

    def stage2(kb, carry):
        k2 = kb * batch
        cols = []
        for q in range(batch):
            zr = zr_scr[pl.ds(k2 + q, n1, stride=n2), :]
            zi = zi_scr[pl.ds(k2 + q, n1, stride=n2), :]
            cols.append(jnp.concatenate([zr, zi], axis=0).astype(BF16))
        rhs = jnp.concatenate(cols, axis=1)
        res = jnp.dot(f2_ref[...], rhs, preferred_element_type=F32)
        for q in range(batch):
            blk = res[:, q * FG_DIM:(q + 1) * FG_DIM]
            xr_scr[pl.ds(k2 + q, n1, stride=n2), :] = blk[:n1]
            xi_scr[pl.ds(k2 + q, n1, stride=n2), :] = blk[n1:]
        return carry

    lax.fori_loop(0, n2 // batch, stage2, 0, unroll=4)

    rows = 1024

    def finish(r, carry):
        r0 = pl.multiple_of(r * rows, rows)
        xri = jnp.concatenate([xr_scr[pl.ds(r0, rows), :], xi_scr[pl.ds(r0, rows), :]], axis=1)
        o = jnp.dot(xri.astype(BF16), gmat, preferred_element_type=F32)
        o_ref[pl.ds(r0, rows), :] = o.astype(o_ref.dtype)
        return carry

    lax.fori_loop(0, SEQ // rows, finish, 0, unroll=2)


def _fourier_lat_call(l, p, w_f, tabs):
    col0 = F_COL // FG_DIM
    const = lambda shape: pl.BlockSpec(shape, lambda g: (0,) * len(shape))
    return pl.pallas_call(
        _fourier_lat_kernel,
        grid=(N_FGROUPS,),
        in_specs=[
            pl.BlockSpec((SEQ, FG_DIM), lambda g: (0, col0 + g)),
            const((FFT_N1, 2 * FFT_N2, FFT_N2)),
            const((2 * FFT_N1, 2 * FFT_N1)),
            const((FG_DIM, FG_DIM)),
            const((FG_DIM, FG_DIM)),
            pl.BlockSpec((1, 1, FG_DIM, FG_DIM), lambda g: (l, g, 0, 0)),
        ],
        out_specs=pl.BlockSpec((SEQ, FG_DIM), lambda g: (0, g)),
        out_shape=jax.ShapeDtypeStruct((T_ALL, D_FOURIER), BF16),
        scratch_shapes=[pltpu.VMEM((SEQ, FG_DIM), F32)] * 4,
        compiler_params=pltpu.CompilerParams(
            dimension_semantics=("arbitrary",), vmem_limit_bytes=VMEM_LIMIT),
        name="fourier_lat",
    )(p, tabs["t1"], tabs["f2"], tabs["cc"], tabs["sc"], w_f)


def _fourier_ctx_kernel(u_ref, cl_ref, sl_ref, cc_ref, sc_ref, w_ref, f_hbm_ref, o_ref):
    del f_hbm_ref
    scale = 1.0 / np.sqrt(float(CTX_LEN) * FG_DIM)
    dot = functools.partial(jnp.dot, precision=HIGHEST, preferred_element_type=F32)
    w = w_ref[0, 0]
    u = u_ref[...]
    a = dot(u, dot(cc_ref[...], w))
    b = dot(u, dot(sc_ref[...], w))
    o_ref[...] = ((dot(cl_ref[...], a) - dot(sl_ref[...], b)) * scale).astype(o_ref.dtype)


def _fourier_ctx_call(l, p, w_f, tabs, f):
    col0 = F_COL // FG_DIM
    row_blk = SEQ // CTX_LEN
    const = lambda shape: pl.BlockSpec(shape, lambda g: (0,) * len(shape))
    return pl.pallas_call(
        _fourier_ctx_kernel,
        grid=(N_FGROUPS,),
        in_specs=[
            pl.BlockSpec((CTX_LEN, FG_DIM), lambda g: (row_blk, col0 + g)),
            const((CTX_LEN, CTX_LEN)),
            const((CTX_LEN, CTX_LEN)),
            const((FG_DIM, FG_DIM)),
            const((FG_DIM, FG_DIM)),
            pl.BlockSpec((1, 1, FG_DIM, FG_DIM), lambda g: (l, g, 0, 0)),
            pl.BlockSpec(memory_space=pl.ANY),
        ],
        out_specs=pl.BlockSpec((CTX_LEN, FG_DIM), lambda g: (row_blk, g)),
        out_shape=jax.ShapeDtypeStruct((T_ALL, D_FOURIER), BF16),
        input_output_aliases={6: 0},
        compiler_params=pltpu.CompilerParams(
            dimension_semantics=("arbitrary",), vmem_limit_bytes=VMEM_LIMIT),
        name="fourier_ctx",
    )(p, tabs["cl"], tabs["sl"], tabs["cc"], tabs["sc"], w_f, f)


def _ssd_prep_kernel(dtr_ref, dtb_ref, alog_ref, tri_ref, cum_ref, tr_ref):
    q = CHUNK
    is_fwd = lax.broadcasted_iota(jnp.int32, (1, DT_PAD), 1) < SSD_HEADS
    neg_a = -jnp.exp(alog_ref[0])
    for c in range(PREP_CHUNKS):
        rs = slice(c * q, (c + 1) * q)
        v = dtr_ref[rs, :] + dtb_ref[0]
        dt = jnp.maximum(v, 0.0) + jnp.log1p(jnp.exp(-jnp.abs(v)))
        dta = dt * neg_a
        run = jnp.dot(tri_ref[0], dta, precision=HIGHEST, preferred_element_type=F32)
        rev = jnp.dot(tri_ref[1], dta, precision=HIGHEST, preferred_element_type=F32)
        cum = jnp.where(is_fwd, run, rev)
        total = jnp.where(is_fwd, run[q - 1:q, :], rev[0:1, :])
        cum2 = cum * LOG2_E
        cum_ref[rs, :] = cum2
        tr_ref[c, 0] = (cum2 - jnp.log2(dt)).T
        tr_ref[c, 1] = (dt * jnp.exp(total - cum)).T


def _ssd_prep_call(l, dtr, dtb, alog, tri):
    layer = lambda width: pl.BlockSpec((1, 1, width), lambda t: (l, 0, 0))
    rows = PREP_CHUNKS * CHUNK
    return pl.pallas_call(
        _ssd_prep_kernel,
        grid=(N_CHUNKS // PREP_CHUNKS,),
        in_specs=[pl.BlockSpec((rows, DT_PAD), lambda t: (t, 0)), layer(DT_PAD), layer(DT_PAD),
                  pl.BlockSpec((2, CHUNK, CHUNK), lambda t: (0, 0, 0))],
        out_specs=[pl.BlockSpec((rows, DT_PAD), lambda t: (t, 0)),
                   pl.BlockSpec((PREP_CHUNKS, 2, DT_PAD, CHUNK), lambda t: (t, 0, 0, 0))],
        out_shape=[jax.ShapeDtypeStruct((T_ALL, DT_PAD), F32),
                   jax.ShapeDtypeStruct((N_CHUNKS, 2, DT_PAD, CHUNK), F32)],
        compiler_params=pltpu.CompilerParams(
            dimension_semantics=("arbitrary",), vmem_limit_bytes=VMEM_LIMIT),
        name="ssd_prep",
    )(dtr, dtb, alog, tri)


def _ssd_kernel(direction, *refs):
    forward = direction == 0
    if forward:
        x_ref, bc_ref, cum_ref, tr_ref, wsrc_ref, o_ref, wdst_ref, h_scr = refs
    else:
        (x_ref, bc_ref, cum_ref, tr_ref, yf_ref, z_ref, dsk_ref, gn_ref, wsrc_ref,
         o_ref, wdst_ref, h_scr) = refs
    wdst_ref[...] = wsrc_ref[0].astype(BF16)
    q = CHUNK
    edge = q - 1 if forward else 0

    @pl.when(pl.program_id(0) == 0)
    def _():
        h_scr[...] = jnp.zeros_like(h_scr)

    cum2 = cum_ref[...]
    cdt_t = tr_ref[0, 0]
    w_t = tr_ref[0, 1]

    x = x_ref[...]
    x16 = x.astype(BF16)

    row = lax.broadcasted_iota(jnp.int32, (q, q), 0)
    col = lax.broadcasted_iota(jnp.int32, (q, q), 1)
    keep = (col <= row) if forward else (col >= row)
    low_lanes = col < SSD_HEAD_DIM

    ys = []
    for g in range(SSD_GROUPS):
        bm = bc_ref[:, g * D_STATE:(g + 1) * D_STATE]
        cm = bc_ref[:, (SSD_GROUPS + g) * D_STATE:(SSD_GROUPS + g + 1) * D_STATE]
        bt = bm.T
        c16 = cm.astype(BF16)
        cb = jnp.dot(c16, bt.astype(BF16), preferred_element_type=F32)
        h_prev = [h_scr[PAIRS_PER_GROUP * g + j] for j in range(PAIRS_PER_GROUP)]
        y_off = jnp.dot(c16, jnp.concatenate(h_prev, axis=1).astype(BF16),
                        preferred_element_type=F32)
        for j in range(PAIRS_PER_GROUP):
            top, bot, ecol = [], [], []
            for r in range(2):
                hh = SSD_HEADS * direction + HEADS_PER_GROUP * g + 2 * j + r
                colb = jnp.broadcast_to(cum2[:, hh:hh + 1], (q, q))
                seg = colb - cdt_t[hh:hh + 1, :]
                decay_dt = jnp.exp2(jnp.where(keep, seg, -jnp.inf))
                top.append((decay_dt * cb).astype(BF16))
                bot.append((bt * w_t[hh:hh + 1, :]).astype(BF16))
                ecol.append(jnp.exp2(colb))
            lhs = jnp.concatenate([jnp.concatenate(top, axis=1),
                                   jnp.concatenate(bot, axis=1)], axis=0)
            c0 = g * GROUP_DIM + j * LANES
            xp = x16[:, c0:c0 + LANES]
            zero = jnp.zeros_like(xp)
            rhs = jnp.concatenate([jnp.where(low_lanes, xp, zero),
                                   jnp.where(low_lanes, zero, xp)], axis=0)
            res = jnp.dot(lhs, rhs, preferred_element_type=F32)
            escale = jnp.where(low_lanes, ecol[0], ecol[1])
            ys.append(res[:q] + y_off[:, j * LANES:(j + 1) * LANES] * escale)
            h_scr[PAIRS_PER_GROUP * g + j] = h_prev[j] * escale[edge:edge + 1, :] + res[q:]
    y = jnp.concatenate(ys, axis=1)

    if forward:
        o_ref[...] = y
    else:
        y = yf_ref[...] + y + dsk_ref[0] * x
        u = y * _silu(z_ref[...])
        outs = []
        for g in range(SSD_GROUPS):
            ug = u[:, g * GROUP_DIM:(g + 1) * GROUP_DIM]
            outs.append(ug * lax.rsqrt(jnp.mean(ug * ug, axis=-1, keepdims=True) + EPS))
        o_ref[...] = (jnp.concatenate(outs, axis=1) * gn_ref[0]).astype(o_ref.dtype)


def _fwd_chunk(t):
    return jnp.where(t < N_CHUNKS - N_LAT_CHUNKS, N_LAT_CHUNKS + t, t - (N_CHUNKS - N_LAT_CHUNKS))


def _bwd_chunk(t):
    return N_CHUNKS - 1 - t


def _ssd_tri():
    lower = np.tril(np.ones((CHUNK, CHUNK), np.float32))
    return jnp.asarray(np.stack([lower, lower.T]))


def _ssd_call(l, direction, p, cum, tr, w_mlp, extra=None):
    chunk_of = _fwd_chunk if direction == 0 else _bwd_chunk
    w_rows, w_cols = w_mlp.shape[1:]
    slab = w_rows // N_LAT_CHUNKS
    slab_of = lambda t: jnp.minimum(t, N_LAT_CHUNKS - 1)
    rows = lambda width, cblk: pl.BlockSpec((CHUNK, width), lambda t: (chunk_of(t), cblk))
    layer = lambda width: pl.BlockSpec((1, 1, width), lambda t: (l, 0, 0))
    bc_width = 2 * SSD_GROUPS * D_STATE
    in_specs = [rows(D_SSD, X_COL // D_SSD), rows(bc_width, BC_COL // bc_width), rows(DT_PAD, 0),
                pl.BlockSpec((1, 2, DT_PAD, CHUNK), lambda t: (chunk_of(t), 0, 0, 0))]
    args = [p, p, cum, tr]
    if direction == 0:
        out_dtype = F32
    else:
        yf, dsk, gn = extra
        in_specs += [rows(D_SSD, 0), rows(D_SSD, Z_COL // D_SSD), layer(D_SSD), layer(D_SSD)]
        args += [yf, p, dsk, gn]
        out_dtype = BF16
    in_specs.append(pl.BlockSpec((1, slab, w_cols), lambda t: (l, slab_of(t), 0)))
    args.append(w_mlp)
    return pl.pallas_call(
        functools.partial(_ssd_kernel, direction),
        grid=(N_CHUNKS,),
        in_specs=in_specs,
        out_specs=[rows(D_SSD, 0), pl.BlockSpec((slab, w_cols), lambda t: (slab_of(t), 0))],
        out_shape=[jax.ShapeDtypeStruct((T_ALL, D_SSD), out_dtype),
                   jax.ShapeDtypeStruct((w_rows, w_cols), BF16)],
        scratch_shapes=[pltpu.VMEM((SSD_GROUPS * PAIRS_PER_GROUP, D_STATE, LANES), F32)],
        compiler_params=pltpu.CompilerParams(
            dimension_semantics=("arbitrary",), vmem_limit_bytes=VMEM_LIMIT),
        name="ssd_fwd" if direction == 0 else "ssd_bwd",
    )(*args)


def _outproj_kernel(x_ref, f_ref, y_ref, w_ref, gt_ref, o_ref, w16_scr):
    i = pl.program_id(0)

    @pl.when(i == 0)
    def _():
        rows = 256

        def body(c, carry):
            rs = pl.ds(pl.multiple_of(c * rows, rows), rows)
            w16_scr[rs, :] = w_ref[0, rs, :].astype(BF16)
            return carry

        lax.fori_loop(0, D_MODEL // rows, body, 0)

    acc = (jnp.dot(f_ref[...], w16_scr[:D_FOURIER, :], preferred_element_type=F32)
           + jnp.dot(y_ref[...], w16_scr[D_FOURIER:, :], preferred_element_type=F32))
    gate = _pick(_is_ctx_rows(i, OUT_ROWS), gt_ref)
    o_ref[...] = x_ref[...] + gate * acc


def _outproj_call(l, xs, f, y, w_out, mod):
    rows = lambda width: pl.BlockSpec((OUT_ROWS, width), lambda i: (i, 0))
    return pl.pallas_call(
        _outproj_kernel,
        grid=(T_ALL // OUT_ROWS,),
        in_specs=[
            rows(D_MODEL), rows(D_FOURIER), rows(D_SSD),
            pl.BlockSpec((1, D_MODEL, D_MODEL), lambda i: (l, 0, 0), pipeline_mode=pl.Buffered(1)),
            pl.BlockSpec((8, D_MODEL), lambda i: (0, 2)),
        ],
        out_specs=rows(D_MODEL),
        out_shape=jax.ShapeDtypeStruct((T_ALL, D_MODEL), F32),
        scratch_shapes=[pltpu.VMEM((D_MODEL, D_MODEL), BF16)],
        compiler_params=pltpu.CompilerParams(
            dimension_semantics=("arbitrary",), vmem_limit_bytes=VMEM_LIMIT),
        name="out_proj",
    )(xs, f, y, w_out, mod)


def _mlp_kernel(final, x_ref, g_ref, sh_ref, sc_ref, gt_ref, w1_ref, w2_ref, gf_ref, o_ref, xn_scr):
    i = pl.program_id(0)
    j = pl.program_id(1)

    @pl.when(j == 0)
    def _():
        _norm_modulate_rows(x_ref, g_ref, sh_ref, sc_ref, xn_scr, i, MLP_ROWS)
        o_ref[...] = jnp.zeros_like(o_ref)

    h = jnp.dot(xn_scr[...], w1_ref[...], preferred_element_type=F32)
    h = jnp.square(jnp.maximum(h, 0.0)).astype(BF16)
    o_ref[...] += jnp.dot(h, w2_ref[...], preferred_element_type=F32)

    @pl.when(j == pl.num_programs(1) - 1)
    def _():
        gate = _pick(_is_ctx_rows(i, MLP_ROWS), gt_ref)
        y = x_ref[...] + gate * o_ref[...]
        if final:
            y = y * lax.rsqrt(jnp.mean(y * y, axis=-1, keepdims=True) + EPS) * gf_ref[...]
        o_ref[...] = y


def _mlp_call(l, xs, g_mlp, mod, w1, w2, g_final, final):
    return pl.pallas_call(
        functools.partial(_mlp_kernel, final),
        grid=(T_ALL // MLP_ROWS, D_FF // COL_TILE),
        in_specs=[
            pl.BlockSpec((MLP_ROWS, D_MODEL), lambda i, j: (i, 0)),
            pl.BlockSpec((1, 1, D_MODEL), lambda i, j: (l, 0, 0)),
            pl.BlockSpec((8, D_MODEL), lambda i, j: (0, 3)),
            pl.BlockSpec((8, D_MODEL), lambda i, j: (0, 4)),
            pl.BlockSpec((8, D_MODEL), lambda i, j: (0, 5)),
            pl.BlockSpec((D_MODEL, COL_TILE), lambda i, j: (0, j)),
            pl.BlockSpec((COL_TILE, D_MODEL), lambda i, j: (j, 0)),
            pl.BlockSpec((1, D_MODEL), lambda i, j: (0, 0)),
        ],
        out_specs=pl.BlockSpec((MLP_ROWS, D_MODEL), lambda i, j: (i, 0)),
        out_shape=jax.ShapeDtypeStruct((SEQ if final else T_ALL, D_MODEL), F32),
        scratch_shapes=[pltpu.VMEM((MLP_ROWS, D_MODEL), BF16)],
        compiler_params=pltpu.CompilerParams(
            dimension_semantics=("arbitrary", "arbitrary"), vmem_limit_bytes=VMEM_LIMIT),
        name="mlp",
    )(xs, g_mlp, mod, mod, mod, w1, w2, g_final)


def _mixer_layer(l, xs, mod, tabs, tri, g_mix, w_in, conv_w, conv_b, dtb, alog, dsk, g_ssd_norm,
                 w_fourier, w_out, w_mlp1, w_mlp2):
    p, dtr = _inproj_call(l, xs, g_mix, mod, w_in, conv_w, conv_b)
    f = _fourier_lat_call(l, p, w_fourier, tabs)
    f = _fourier_ctx_call(l, p, w_fourier, tabs, f)
    cum, tr = _ssd_prep_call(l, dtr, dtb, alog, tri)
    yf, w1 = _ssd_call(l, 0, p, cum, tr, w_mlp1)
    y, w2 = _ssd_call(l, 1, p, cum, tr, w_mlp2, extra=(yf, dsk, g_ssd_norm))
    return _outproj_call(l, xs, f, y, w_out, mod), w1, w2


def kernel(x, c, ctx, c_ctx, w_ada, b_ada, g_mix, w_in, conv_w, conv_b, dt_bias, a_log, d_skip,
           g_ssd_norm, w_fourier, w_out, g_mlp, w_mlp1, w_mlp2, g_final):
    assert x.shape == (1, SEQ, D_MODEL) and ctx.shape == (1, CTX_LEN, D_MODEL)
    tabs = _dft_tables()
    tri = _ssd_tri()

    xs = jnp.concatenate([x[0], ctx[0]], axis=0)
    cc = jnp.concatenate([c, c_ctx[None, :], jnp.zeros((ADA_ROWS - 2, D_MODEL), F32)], axis=0)
    mods = _ada_call(cc, w_ada, b_ada)

    pad_heads = ((0, 0), (0, 0), (0, DT_PAD - 2 * SSD_HEADS))
    dtb = jnp.pad(dt_bias.reshape(DEPTH, 1, 2 * SSD_HEADS), pad_heads)
    alog = jnp.pad(a_log.reshape(DEPTH, 1, 2 * SSD_HEADS), pad_heads)
    dsk = jnp.repeat(d_skip, SSD_HEAD_DIM, axis=1).reshape(DEPTH, 1, D_SSD)
    row3 = lambda a: a.reshape(DEPTH, 1, a.shape[-1])
    w_in_t = jnp.swapaxes(w_in, 1, 2)

    for l in range(DEPTH):
        xs, w1, w2 = _mixer_layer(l, xs, mods[l], tabs, tri, row3(g_mix), w_in_t, conv_w, row3(conv_b),
                                  dtb, alog, dsk, row3(g_ssd_norm), w_fourier, w_out, w_mlp1, w_mlp2)
        xs = _mlp_call(l, xs, row3(g_mlp), mods[l], w1, w2, g_final[None, :], final=l == DEPTH - 1)

    return xs[None]
```

```python
import functools

import numpy as np
import jax
import jax.numpy as jnp
from jax import lax
from jax.experimental import pallas as pl
from jax.experimental.pallas import tpu as pltpu

F32 = jnp.float32
BF16 = jnp.bfloat16
HIGHEST = lax.Precision.HIGHEST

D_MODEL = 2048
SEQ = 8192
CTX_LEN = 256
T_ALL = SEQ + CTX_LEN
DEPTH = 4
GRID_W = 64
D_FOURIER = 512
N_FGROUPS = 4
FG_DIM = 128
D_SSD = 1536
SSD_HEAD_DIM = 64
SSD_HEADS = 24
SSD_GROUPS = 4
HEADS_PER_GROUP = 6
PAIRS_PER_GROUP = HEADS_PER_GROUP // 2
GROUP_DIM = D_SSD // SSD_GROUPS
D_STATE = 128
D_CONV = 5
CONV_DIM = 2560
CHUNK = 128
N_CHUNKS = T_ALL // CHUNK
N_LAT_CHUNKS = SEQ // CHUNK
STEP_CHUNKS = CTX_LEN // CHUNK
N_SSD_STEPS = N_CHUNKS // STEP_CHUNKS
D_MAIN = D_FOURIER + D_SSD + CONV_DIM
D_IN_PROJ = D_MAIN + 2 * SSD_HEADS
DT_PAD = 128
D_FF = 4 * D_MODEL
EPS = 1e-6
LOG2_E = float(np.log2(np.e))

SUBLANES = 8
LANES = 128
COL_TILE = 512
MLP_ROWS = 768
PROJ_ROWS = 1408
OUT_ROWS = 528
ADA_ROWS = 16
ROW_CHUNK = 128
PREP_CHUNKS = 6
FFT_N1 = 64
FFT_N2 = 128
VMEM_LIMIT = 56 * 1024 * 1024

N_PROJ_TILES = D_MAIN // COL_TILE
N_PROJ_ROW_BLOCKS = T_ALL // PROJ_ROWS
N_PROJ_STEPS = N_PROJ_ROW_BLOCKS * N_PROJ_TILES
FIRST_XBC_TILE = (D_FOURIER + D_SSD) // COL_TILE
Z_COL = 0
X_COL = D_SSD
BC_COL = 2 * D_SSD
F_COL = D_SSD + CONV_DIM


def _silu(v):
    return v * jax.nn.sigmoid(v)


def _is_ctx_rows(block_idx, rows_per_block):
    row = block_idx * rows_per_block + lax.broadcasted_iota(jnp.int32, (rows_per_block, 1), 0)
    return row >= SEQ


def _pick(is_ctx, ref):
    return jnp.where(is_ctx, ref[1:2, :], ref[0:1, :])


def _norm_modulate_rows(x_ref, g_ref, sh_ref, sc_ref, xn_scr, block_idx, rows_per_block):
    g = g_ref[0]

    def body(c, carry):
        r0 = pl.multiple_of(c * ROW_CHUNK, ROW_CHUNK)
        is_ctx = block_idx * rows_per_block + r0 >= SEQ
        x = x_ref[pl.ds(r0, ROW_CHUNK), :]
        y = x * lax.rsqrt(jnp.mean(x * x, axis=-1, keepdims=True) + EPS) * g
        y = y * (1.0 + _pick(is_ctx, sc_ref)) + _pick(is_ctx, sh_ref)
        xn_scr[pl.ds(r0, ROW_CHUNK), :] = y.astype(BF16)
        return carry

    lax.fori_loop(0, rows_per_block // ROW_CHUNK, body, 0)


def _ada_kernel(c_ref, w_ref, b_ref, o_ref):
    s = _silu(c_ref[...]).astype(BF16)
    o_ref[0] = jnp.dot(s, w_ref[0].astype(BF16), preferred_element_type=F32) + b_ref[0]


def _ada_call(cc, w_ada, b_ada):
    tn = 1024
    return pl.pallas_call(
        _ada_kernel,
        grid=(DEPTH, 6 * D_MODEL // tn),
        in_specs=[
            pl.BlockSpec((ADA_ROWS, D_MODEL), lambda l, j: (0, 0)),
            pl.BlockSpec((1, D_MODEL, tn), lambda l, j: (l, 0, j)),
            pl.BlockSpec((1, 1, tn), lambda l, j: (l, 0, j)),
        ],
        out_specs=pl.BlockSpec((1, ADA_ROWS, tn), lambda l, j: (l, 0, j)),
        out_shape=jax.ShapeDtypeStruct((DEPTH, ADA_ROWS, 6 * D_MODEL), F32),
        compiler_params=pltpu.CompilerParams(
            dimension_semantics=("arbitrary", "arbitrary"), vmem_limit_bytes=VMEM_LIMIT),
        name="ada_mod",
    )(cc, w_ada, b_ada.reshape(DEPTH, 1, 6 * D_MODEL))


def _conv_silu(v, w_ref, b_ref, group_rows, col_slice):
    rows, cols = v.shape
    tiles = group_rows // SUBLANES
    v4 = v.reshape(rows // group_rows, tiles, SUBLANES, cols)
    sub = lax.broadcasted_iota(jnp.int32, (1, 1, SUBLANES, cols), 2)
    zero_tile = jnp.zeros((rows // group_rows, 1, SUBLANES, cols), F32)
    half = D_CONV // 2
    w_ref = w_ref.at[:, :, col_slice]
    b_ref = b_ref.at[:, :, col_slice]
    acc = v4 * w_ref[0, half:half + 1, :]
    for k in range(D_CONV):
        off = k - half
        if off == 0:
            continue
        r = pltpu.roll(v4, (-off) % SUBLANES, axis=2)
        if off < 0:
            nbr = jnp.concatenate([zero_tile, r[:, :-1]], axis=1)
            shifted = jnp.where(sub < -off, nbr, r)
        else:
            nbr = jnp.concatenate([r[:, 1:], zero_tile], axis=1)
            shifted = jnp.where(sub < SUBLANES - off, r, nbr)
        acc = acc + shifted * w_ref[0, k:k + 1, :]
    return _silu(acc + b_ref[0]).reshape(rows, cols)


def _dot_nt(a, b_t):
    return lax.dot_general(a, b_t, (((1,), (1,)), ((), ())), preferred_element_type=F32)


def _inproj_kernel(x_ref, g_ref, sh_ref, sc_ref, w_ref, wdt_ref, cw_ref, cb_ref, p_ref, dt_ref, xn_scr):
    i = pl.program_id(0)
    j = pl.program_id(1)
    last = pl.num_programs(0) - 1

    @pl.when(j == 0)
    def _():
        _norm_modulate_rows(x_ref, g_ref, sh_ref, sc_ref, xn_scr, i, PROJ_ROWS)
        row = lax.broadcasted_iota(jnp.int32, (DT_PAD, 1), 0)
        wdt = jnp.where(row < 2 * SSD_HEADS, wdt_ref[0], 0.0).astype(BF16)
        dt_ref[...] = _dot_nt(xn_scr[...], wdt)

    def matmul():
        return jnp.dot(xn_scr[...], w_ref[0].T.astype(BF16), preferred_element_type=F32)

    all_cols = slice(0, COL_TILE)

    @pl.when(j < FIRST_XBC_TILE)
    def _():
        p_ref[...] = matmul()

    @pl.when(jnp.logical_and(j >= FIRST_XBC_TILE, i < last))
    def _():
        w16 = w_ref[0].T.astype(BF16)
        bounds = tuple(range(0, PROJ_ROWS + 1, 2 * GRID_W))
        for r0, r1 in zip(bounds[:-1], bounds[1:]):
            acc = jnp.dot(xn_scr[r0:r1, :], w16, preferred_element_type=F32)
            p_ref[r0:r1, :] = _conv_silu(acc, cw_ref, cb_ref, GRID_W, all_cols)

    @pl.when(jnp.logical_and(j >= FIRST_XBC_TILE, i == last))
    def _():
        n_lat = PROJ_ROWS - CTX_LEN
        acc = matmul()
        p_ref[:n_lat, :] = _conv_silu(acc[:n_lat], cw_ref, cb_ref, GRID_W, all_cols)
        p_ref[n_lat:, :] = _conv_silu(acc[n_lat:], cw_ref, cb_ref, CTX_LEN, all_cols)


def _proj_col_block(j):
    return jnp.where(j == 0, N_PROJ_TILES - 1, j - 1)


def _inproj_call(l, xs, g_mix, mod, w_in, conv_w, conv_b):
    xbc_tile = lambda j: jnp.maximum(j - FIRST_XBC_TILE, 0)
    return pl.pallas_call(
        _inproj_kernel,
        grid=(N_PROJ_ROW_BLOCKS, N_PROJ_TILES),
        in_specs=[
            pl.BlockSpec((PROJ_ROWS, D_MODEL), lambda i, j: (i, 0)),
            pl.BlockSpec((1, 1, D_MODEL), lambda i, j: (l, 0, 0)),
            pl.BlockSpec((8, D_MODEL), lambda i, j: (0, 0)),
            pl.BlockSpec((8, D_MODEL), lambda i, j: (0, 1)),
            pl.BlockSpec((1, COL_TILE, D_MODEL), lambda i, j: (l, j, 0)),
            pl.BlockSpec((1, DT_PAD, D_MODEL), lambda i, j: (l, D_MAIN // DT_PAD, 0)),
            pl.BlockSpec((1, D_CONV, COL_TILE), lambda i, j: (l, 0, xbc_tile(j))),
            pl.BlockSpec((1, 1, COL_TILE), lambda i, j: (l, 0, xbc_tile(j))),
        ],
        out_specs=[
            pl.BlockSpec((PROJ_ROWS, COL_TILE), lambda i, j: (i, _proj_col_block(j))),
            pl.BlockSpec((PROJ_ROWS, DT_PAD), lambda i, j: (i, 0)),
        ],
        out_shape=[
            jax.ShapeDtypeStruct((T_ALL, D_MAIN), F32),
            jax.ShapeDtypeStruct((T_ALL, DT_PAD), F32),
        ],
        scratch_shapes=[pltpu.VMEM((PROJ_ROWS, D_MODEL), BF16)],
        compiler_params=pltpu.CompilerParams(
            dimension_semantics=("arbitrary", "arbitrary"), vmem_limit_bytes=VMEM_LIMIT),
        name="in_proj",
    )(xs, g_mix, mod, mod, w_in, w_in, conv_w, conv_b)


def _dft_tables():
    L, n1, n2 = SEQ, FFT_N1, FFT_N2
    a = np.arange(n1)[:, None, None]
    k2 = np.arange(n2)[None, :, None]
    b = np.arange(n2)[None, None, :]
    ang = 2.0 * np.pi * ((k2 * (a + n1 * b)) % L) / L
    t1 = np.concatenate([np.cos(ang), -np.sin(ang)], axis=1)
    k1 = np.arange(n1)[:, None]
    aa = np.arange(n1)[None, :]
    ang2 = 2.0 * np.pi * ((k1 * aa) % n1) / n1
    c2, s2 = np.cos(ang2), np.sin(ang2)
    f2 = np.block([[c2, s2], [-s2, c2]])
    cc = np.arange(FG_DIM)
    angc = 2.0 * np.pi * ((cc[:, None] * cc[None, :]) % FG_DIM) / FG_DIM
    lc = np.arange(CTX_LEN)
    angl = 2.0 * np.pi * ((lc[:, None] * lc[None, :]) % CTX_LEN) / CTX_LEN
    as32 = lambda v: jnp.asarray(v, dtype=F32)
    as16 = lambda v: jnp.asarray(v, dtype=F32).astype(BF16)
    return dict(t1=as16(t1), f2=as16(f2), cc=as32(np.cos(angc)), sc=as32(np.sin(angc)),
                cl=as32(np.cos(angl)), sl=as32(np.sin(angl)))


def _fourier_lat_kernel(u_ref, t1_ref, f2_ref, cc_ref, sc_ref, w_ref, o_ref,
                        zr_scr, zi_scr, xr_scr, xi_scr):
    n1, n2 = FFT_N1, FFT_N2
    scale = 1.0 / np.sqrt(float(SEQ) * FG_DIM)
    w = w_ref[0, 0]
    g1 = jnp.dot(cc_ref[...], w, precision=HIGHEST, preferred_element_type=F32) * scale
    g2 = jnp.dot(sc_ref[...], w, precision=HIGHEST, preferred_element_type=F32) * scale
    gmat = jnp.concatenate([g1, g2], axis=0).astype(BF16)

    def stage1(a, carry):
        xa = u_ref[pl.ds(a, n2, stride=n1), :].astype(BF16)
        z = jnp.dot(t1_ref[a], xa, preferred_element_type=F32)
        row0 = pl.multiple_of(a * n2, n2)
        zr_scr[pl.ds(row0, n2), :] = z[:n2]
        zi_scr[pl.ds(row0, n2), :] = z[n2:]
        return carry

    lax.fori_loop(0, n1, stage1, 0, unroll=8)

    batch = 4

    def stage2(kb, carry):
        k2 = kb * batch
        cols = []
        for q in range(batch):
            zr = zr_scr[pl.ds(k2 + q, n1, stride=n2), :]
            zi = zi_scr[pl.ds(k2 + q, n1, stride=n2), :]
            cols.append(jnp.concatenate([zr, zi], axis=0).astype(BF16))
        rhs = jnp.concatenate(cols, axis=1)
        res = jnp.dot(f2_ref[...], rhs, preferred_element_type=F32)
        for q in range(batch):
            blk = res[:, q * FG_DIM:(q + 1) * FG_DIM]
            row0 = pl.multiple_of((k2 + q) * n1, n1)
            xr_scr[pl.ds(row0, n1), :] = blk[:n1]
            xi_scr[pl.ds(row0, n1), :] = blk[n1:]
        return carry

    lax.fori_loop(0, n2 // batch, stage2, 0, unroll=4)

    def finish(k1, carry):
        xr = xr_scr[pl.ds(k1, n2, stride=n1), :]
        xi = xi_scr[pl.ds(k1, n2, stride=n1), :]
        xri = jnp.concatenate([xr, xi], axis=1).astype(BF16)
        o = jnp.dot(xri, gmat, preferred_element_type=F32)
        o_ref[pl.ds(pl.multiple_of(k1 * n2, n2), n2), :] = o.astype(o_ref.dtype)
        return carry

    lax.fori_loop(0, n1, finish, 0, unroll=8)


def _fourier_lat_call(l, p, w_f, tabs):
    col0 = F_COL // FG_DIM
    const = lambda shape: pl.BlockSpec(shape, lambda g: (0,) * len(shape))
    return pl.pallas_call(
        _fourier_lat_kernel,
        grid=(N_FGROUPS,),
        in_specs=[
            pl.BlockSpec((SEQ, FG_DIM), lambda g: (0, col0 + g)),
            const((FFT_N1, 2 * FFT_N2, FFT_N2)),
            const((2 * FFT_N1, 2 * FFT_N1)),
            const((FG_DIM, FG_DIM)),
            const((FG_DIM, FG_DIM)),
            pl.BlockSpec((1, 1, FG_DIM, FG_DIM), lambda g: (l, g, 0, 0)),
        ],
        out_specs=pl.BlockSpec((SEQ, FG_DIM), lambda g: (0, g)),
        out_shape=jax.ShapeDtypeStruct((T_ALL, D_FOURIER), BF16),
        scratch_shapes=[pltpu.VMEM((SEQ, FG_DIM), F32)] * 4,
        compiler_params=pltpu.CompilerParams(
            dimension_semantics=("arbitrary",), vmem_limit_bytes=VMEM_LIMIT),
        name="fourier_lat",
    )(p, tabs["t1"], tabs["f2"], tabs["cc"], tabs["sc"], w_f)


def _fourier_ctx_kernel(u_ref, cl_ref, sl_ref, cc_ref, sc_ref, w_ref, f_hbm_ref, o_ref):
    del f_hbm_ref
    scale = 1.0 / np.sqrt(float(CTX_LEN) * FG_DIM)
    dot = functools.partial(jnp.dot, precision=HIGHEST, preferred_element_type=F32)
    w = w_ref[0, 0]
    u = u_ref[...]
    a = dot(u, dot(cc_ref[...], w))
    b = dot(u, dot(sc_ref[...], w))
    o_ref[...] = ((dot(cl_ref[...], a) - dot(sl_ref[...], b)) * scale).astype(o_ref.dtype)


def _fourier_ctx_call(l, p, w_f, tabs, f):
    col0 = F_COL // FG_DIM
    row_blk = SEQ // CTX_LEN
    const = lambda shape: pl.BlockSpec(shape, lambda g: (0,) * len(shape))
    return pl.pallas_call(
        _fourier_ctx_kernel,
        grid=(N_FGROUPS,),
        in_specs=[
            pl.BlockSpec((CTX_LEN, FG_DIM), lambda g: (row_blk, col0 + g)),
            const((CTX_LEN, CTX_LEN)),
            const((CTX_LEN, CTX_LEN)),
            const((FG_DIM, FG_DIM)),
            const((FG_DIM, FG_DIM)),
            pl.BlockSpec((1, 1, FG_DIM, FG_DIM), lambda g: (l, g, 0, 0)),
            pl.BlockSpec(memory_space=pl.ANY),
        ],
        out_specs=pl.BlockSpec((CTX_LEN, FG_DIM), lambda g: (row_blk, g)),
        out_shape=jax.ShapeDtypeStruct((T_ALL, D_FOURIER), BF16),
        input_output_aliases={6: 0},
        compiler_params=pltpu.CompilerParams(
            dimension_semantics=("arbitrary",), vmem_limit_bytes=VMEM_LIMIT),
        name="fourier_ctx",
    )(p, tabs["cl"], tabs["sl"], tabs["cc"], tabs["sc"], w_f, f)


def _ssd_prep_kernel(dtr_ref, dtb_ref, alog_ref, tri_ref, cum_ref, tr_ref):
    q = CHUNK
    is_fwd = lax.broadcasted_iota(jnp.int32, (1, DT_PAD), 1) < SSD_HEADS
    neg_a = -jnp.exp(alog_ref[0])
    for c in range(PREP_CHUNKS):
        rs = slice(c * q, (c + 1) * q)
        v = dtr_ref[rs, :] + dtb_ref[0]
        dt = jnp.maximum(v, 0.0) + jnp.log1p(jnp.exp(-jnp.abs(v)))
        dta = dt * neg_a
        run = jnp.dot(tri_ref[0], dta, precision=HIGHEST, preferred_element_type=F32)
        rev = jnp.dot(tri_ref[1], dta, precision=HIGHEST, preferred_element_type=F32)
        cum = jnp.where(is_fwd, run, rev)
        total = jnp.where(is_fwd, run[q - 1:q, :], rev[0:1, :])
        cum2 = cum * LOG2_E
        cum_ref[rs, :] = cum2
        tr_ref[c, 0] = (cum2 - jnp.log2(dt)).T
        tr_ref[c, 1] = (dt * jnp.exp(total - cum)).T


def _ssd_prep_call(l, dtr, dtb, alog, tri):
    layer = lambda width: pl.BlockSpec((1, 1, width), lambda t: (l, 0, 0))
    rows = PREP_CHUNKS * CHUNK
    return pl.pallas_call(
        _ssd_prep_kernel,
        grid=(N_CHUNKS // PREP_CHUNKS,),
        in_specs=[pl.BlockSpec((rows, DT_PAD), lambda t: (t, 0)), layer(DT_PAD), layer(DT_PAD),
                  pl.BlockSpec((2, CHUNK, CHUNK), lambda t: (0, 0, 0))],
        out_specs=[pl.BlockSpec((rows, DT_PAD), lambda t: (t, 0)),
                   pl.BlockSpec((PREP_CHUNKS, 2, DT_PAD, CHUNK), lambda t: (t, 0, 0, 0))],
        out_shape=[jax.ShapeDtypeStruct((T_ALL, DT_PAD), F32),
                   jax.ShapeDtypeStruct((N_CHUNKS, 2, DT_PAD, CHUNK), F32)],
        compiler_params=pltpu.CompilerParams(
            dimension_semantics=("arbitrary",), vmem_limit_bytes=VMEM_LIMIT),
        name="ssd_prep",
    )(dtr, dtb, alog, tri)


def _ssd_kernel(direction, *refs):
    forward = direction == 0
    if forward:
        x_ref, bc_ref, cum_ref, tr_ref, wsrc_ref, o_ref, wdst_ref, h_scr = refs
    else:
        (x_ref, bc_ref, cum_ref, tr_ref, yf_ref, z_ref, dsk_ref, gn_ref, wsrc_ref,
         o_ref, wdst_ref, h_scr) = refs
    wdst_ref[...] = wsrc_ref[0].astype(BF16)
    q = CHUNK
    edge = q - 1 if forward else 0

    @pl.when(pl.program_id(0) == 0)
    def _():
        h_scr[...] = jnp.zeros_like(h_scr)

    row = lax.broadcasted_iota(jnp.int32, (q, q), 0)
    col = lax.broadcasted_iota(jnp.int32, (q, q), 1)
    keep = (col <= row) if forward else (col >= row)
    low_lanes = col < SSD_HEAD_DIM

    for c in (range(STEP_CHUNKS) if forward else reversed(range(STEP_CHUNKS))):
        rs = slice(c * q, (c + 1) * q)
        cum2 = cum_ref[rs, :]
        cdt_t = tr_ref[c, 0]
        w_t = tr_ref[c, 1]
        x = x_ref[rs, :]
        x16 = x.astype(BF16)

        ys = []
        for g in range(SSD_GROUPS):
            bm = bc_ref[rs, g * D_STATE:(g + 1) * D_STATE]
            cm = bc_ref[rs, (SSD_GROUPS + g) * D_STATE:(SSD_GROUPS + g + 1) * D_STATE]
            bt = bm.T
            c16 = cm.astype(BF16)
            cb = jnp.dot(c16, bt.astype(BF16), preferred_element_type=F32)
            h_prev = [h_scr[PAIRS_PER_GROUP * g + j] for j in range(PAIRS_PER_GROUP)]
            y_off = jnp.dot(c16, jnp.concatenate(h_prev, axis=1).astype(BF16),
                            preferred_element_type=F32)
            for j in range(PAIRS_PER_GROUP):
                top, bot, ecol = [], [], []
                for r in range(2):
                    hh = SSD_HEADS * direction + HEADS_PER_GROUP * g + 2 * j + r
                    colb = jnp.broadcast_to(cum2[:, hh:hh + 1], (q, q))
                    seg = colb - cdt_t[hh:hh + 1, :]
                    decay_dt = jnp.exp2(jnp.where(keep, seg, -jnp.inf))
                    top.append((decay_dt * cb).astype(BF16))
                    bot.append((bt * w_t[hh:hh + 1, :]).astype(BF16))
                    ecol.append(jnp.exp2(colb))
                lhs = jnp.concatenate([jnp.concatenate(top, axis=1),
                                       jnp.concatenate(bot, axis=1)], axis=0)
                c0 = g * GROUP_DIM + j * LANES
                xp = x16[:, c0:c0 + LANES]
                zero = jnp.zeros_like(xp)
                rhs = jnp.concatenate([jnp.where(low_lanes, xp, zero),
                                       jnp.where(low_lanes, zero, xp)], axis=0)
                res = jnp.dot(lhs, rhs, preferred_element_type=F32)
                escale = jnp.where(low_lanes, ecol[0], ecol[1])
                ys.append(res[:q] + y_off[:, j * LANES:(j + 1) * LANES] * escale)
                h_scr[PAIRS_PER_GROUP * g + j] = h_prev[j] * escale[edge:edge + 1, :] + res[q:]
        y = jnp.concatenate(ys, axis=1)

        if forward:
            o_ref[rs, :] = y
        else:
            y = yf_ref[rs, :] + y + dsk_ref[0] * x
            u = y * _silu(z_ref[rs, :])
            outs = []
            for g in range(SSD_GROUPS):
                ug = u[:, g * GROUP_DIM:(g + 1) * GROUP_DIM]
                outs.append(ug * lax.rsqrt(jnp.mean(ug * ug, axis=-1, keepdims=True) + EPS))
            o_ref[rs, :] = (jnp.concatenate(outs, axis=1) * gn_ref[0]).astype(o_ref.dtype)


def _fwd_block(t):
    return jnp.where(t == 0, N_SSD_STEPS - 1, t - 1)


def _bwd_block(t):
    return N_SSD_STEPS - 1 - t


def _ssd_tri():
    lower = np.tril(np.ones((CHUNK, CHUNK), np.float32))
    return jnp.asarray(np.stack([lower, lower.T]))


def _ssd_call(l, direction, p, cum, tr, w_mlp, extra=None):
    block_of = _fwd_block if direction == 0 else _bwd_block
    w_rows, w_cols = w_mlp.shape[1:]
    cast_steps = N_SSD_STEPS - 1
    slab = w_rows // cast_steps
    slab_of = lambda t: jnp.minimum(t, cast_steps - 1)
    rows = lambda width, cblk: pl.BlockSpec((STEP_CHUNKS * CHUNK, width), lambda t: (block_of(t), cblk))
    layer = lambda width: pl.BlockSpec((1, 1, width), lambda t: (l, 0, 0))
    bc_width = 2 * SSD_GROUPS * D_STATE
    in_specs = [rows(D_SSD, X_COL // D_SSD), rows(bc_width, BC_COL // bc_width), rows(DT_PAD, 0),
                pl.BlockSpec((STEP_CHUNKS, 2, DT_PAD, CHUNK), lambda t: (block_of(t), 0, 0, 0))]
    args = [p, p, cum, tr]
    if direction == 0:
        out_dtype = F32
    else:
        yf, dsk, gn = extra
        in_specs += [rows(D_SSD, 0), rows(D_SSD, Z_COL // D_SSD), layer(D_SSD), layer(D_SSD)]
        args += [yf, p, dsk, gn]
        out_dtype = BF16
    in_specs.append(pl.BlockSpec((1, slab, w_cols), lambda t: (l, slab_of(t), 0)))
    args.append(w_mlp)
    return pl.pallas_call(
        functools.partial(_ssd_kernel, direction),
        grid=(N_SSD_STEPS,),
        in_specs=in_specs,
        out_specs=[rows(D_SSD, 0), pl.BlockSpec((slab, w_cols), lambda t: (slab_of(t), 0))],
        out_shape=[jax.ShapeDtypeStruct((T_ALL, D_SSD), out_dtype),
                   jax.ShapeDtypeStruct((w_rows, w_cols), BF16)],
        scratch_shapes=[pltpu.VMEM((SSD_GROUPS * PAIRS_PER_GROUP, D_STATE, LANES), F32)],
        compiler_params=pltpu.CompilerParams(
            dimension_semantics=("arbitrary",), vmem_limit_bytes=VMEM_LIMIT),
        name="ssd_fwd" if direction == 0 else "ssd_bwd",
    )(*args)


def _outproj_kernel(x_ref, f_ref, y_ref, w_ref, gt_ref, o_ref, w16_scr):
    i = pl.program_id(0)

    @pl.when(i == 0)
    def _():
        rows = 256

        def body(c, carry):
            rs = pl.ds(pl.multiple_of(c * rows, rows), rows)
            w16_scr[rs, :] = w_ref[0, rs, :].astype(BF16)
            return carry

        lax.fori_loop(0, D_MODEL // rows, body, 0)

    acc = (jnp.dot(f_ref[...], w16_scr[:D_FOURIER, :], preferred_element_type=F32)
           + jnp.dot(y_ref[...], w16_scr[D_FOURIER:, :], preferred_element_type=F32))
    gate = _pick(_is_ctx_rows(i, OUT_ROWS), gt_ref)
    o_ref[...] = x_ref[...] + gate * acc


def _outproj_call(l, xs, f, y, w_out, mod):
    rows = lambda width: pl.BlockSpec((OUT_ROWS, width), lambda i: (i, 0))
    return pl.pallas_call(
        _outproj_kernel,
        grid=(T_ALL // OUT_ROWS,),
        in_specs=[
            rows(D_MODEL), rows(D_FOURIER), rows(D_SSD),
            pl.BlockSpec((1, D_MODEL, D_MODEL), lambda i: (l, 0, 0), pipeline_mode=pl.Buffered(1)),
            pl.BlockSpec((8, D_MODEL), lambda i: (0, 2)),
        ],
        out_specs=rows(D_MODEL),
        out_shape=jax.ShapeDtypeStruct((T_ALL, D_MODEL), F32),
        scratch_shapes=[pltpu.VMEM((D_MODEL, D_MODEL), BF16)],
        compiler_params=pltpu.CompilerParams(
            dimension_semantics=("arbitrary",), vmem_limit_bytes=VMEM_LIMIT),
        name="out_proj",
    )(xs, f, y, w_out, mod)


def _mlp_kernel(final, x_ref, g_ref, sh_ref, sc_ref, gt_ref, w1_ref, w2_ref, gf_ref, o_ref, xn_scr):
    i = pl.program_id(0)
    j = pl.program_id(1)

    @pl.when(j == 0)
    def _():
        _norm_modulate_rows(x_ref, g_ref, sh_ref, sc_ref, xn_scr, i, MLP_ROWS)
        o_ref[...] = jnp.zeros_like(o_ref)

    h = jnp.dot(xn_scr[...], w1_ref[...], preferred_element_type=F32)
    h = jnp.square(jnp.maximum(h, 0.0)).astype(BF16)
    o_ref[...] += jnp.dot(h, w2_ref[...], preferred_element_type=F32)

    @pl.when(j == pl.num_programs(1) - 1)
    def _():
        gate = _pick(_is_ctx_rows(i, MLP_ROWS), gt_ref)
        y = x_ref[...] + gate * o_ref[...]
        if final:
            y = y * lax.rsqrt(jnp.mean(y * y, axis=-1, keepdims=True) + EPS) * gf_ref[...]
        o_ref[...] = y


def _mlp_call(l, xs, g_mlp, mod, w1, w2, g_final, final):
    return pl.pallas_call(
        functools.partial(_mlp_kernel, final),
        grid=(T_ALL // MLP_ROWS, D_FF // COL_TILE),
        in_specs=[
            pl.BlockSpec((MLP_ROWS, D_MODEL), lambda i, j: (i, 0)),
            pl.BlockSpec((1, 1, D_MODEL), lambda i, j: (l, 0, 0)),
            pl.BlockSpec((8, D_MODEL), lambda i, j: (0, 3)),
            pl.BlockSpec((8, D_MODEL), lambda i, j: (0, 4)),
            pl.BlockSpec((8, D_MODEL), lambda i, j: (0, 5)),
            pl.BlockSpec((D_MODEL, COL_TILE), lambda i, j: (0, j)),
            pl.BlockSpec((COL_TILE, D_MODEL), lambda i, j: (j, 0)),
            pl.BlockSpec((1, D_MODEL), lambda i, j: (0, 0)),
        ],
        out_specs=pl.BlockSpec((MLP_ROWS, D_MODEL), lambda i, j: (i, 0)),
        out_shape=jax.ShapeDtypeStruct((SEQ if final else T_ALL, D_MODEL), F32),
        scratch_shapes=[pltpu.VMEM((MLP_ROWS, D_MODEL), BF16)],
        compiler_params=pltpu.CompilerParams(
            dimension_semantics=("arbitrary", "arbitrary"), vmem_limit_bytes=VMEM_LIMIT),
        name="mlp",
    )(xs, g_mlp, mod, mod, mod, w1, w2, g_final)


def _mixer_layer(l, xs, mod, tabs, tri, g_mix, w_in, conv_w, conv_b, dtb, alog, dsk, g_ssd_norm,
                 w_fourier, w_out, w_mlp1, w_mlp2):
    p, dtr = _inproj_call(l, xs, g_mix, mod, w_in, conv_w, conv_b)
    f = _fourier_lat_call(l, p, w_fourier, tabs)
    f = _fourier_ctx_call(l, p, w_fourier, tabs, f)
    cum, tr = _ssd_prep_call(l, dtr, dtb, alog, tri)
    yf, w1 = _ssd_call(l, 0, p, cum, tr, w_mlp1)
    y, w2 = _ssd_call(l, 1, p, cum, tr, w_mlp2, extra=(yf, dsk, g_ssd_norm))
    return _outproj_call(l, xs, f, y, w_out, mod), w1, w2


def kernel(x, c, ctx, c_ctx, w_ada, b_ada, g_mix, w_in, conv_w, conv_b, dt_bias, a_log, d_skip,
           g_ssd_norm, w_fourier, w_out, g_mlp, w_mlp1, w_mlp2, g_final):
    assert x.shape == (1, SEQ, D_MODEL) and ctx.shape == (1, CTX_LEN, D_MODEL)
    tabs = _dft_tables()
    tri = _ssd_tri()

    xs = jnp.concatenate([x[0], ctx[0]], axis=0)
    cc = jnp.concatenate([c, c_ctx[None, :], jnp.zeros((ADA_ROWS - 2, D_MODEL), F32)], axis=0)
    mods = _ada_call(cc, w_ada, b_ada)

    pad_heads = ((0, 0), (0, 0), (0, DT_PAD - 2 * SSD_HEADS))
    dtb = jnp.pad(dt_bias.reshape(DEPTH, 1, 2 * SSD_HEADS), pad_heads)
    alog = jnp.pad(a_log.reshape(DEPTH, 1, 2 * SSD_HEADS), pad_heads)
    dsk = jnp.repeat(d_skip, SSD_HEAD_DIM, axis=1).reshape(DEPTH, 1, D_SSD)
    row3 = lambda a: a.reshape(DEPTH, 1, a.shape[-1])
    w_in_t = jnp.swapaxes(w_in, 1, 2)

    for l in range(DEPTH):
        xs, w1, w2 = _mixer_layer(l, xs, mods[l], tabs, tri, row3(g_mix), w_in_t, conv_w, row3(conv_b),
                                  dtb, alog, dsk, row3(g_ssd_norm), w_fourier, w_out, w_mlp1, w_mlp2)
        xs = _mlp_call(l, xs, row3(g_mlp), mods[l], w1, w2, g_final[None, :], final=l == DEPTH - 1)

    return xs[None]
```

```python
import functools

import numpy as np
import jax
import jax.numpy as jnp
from jax import lax
from jax.experimental import pallas as pl
from jax.experimental.pallas import tpu as pltpu

F32 = jnp.float32
BF16 = jnp.bfloat16
HIGHEST = lax.Precision.HIGHEST

D_MODEL = 2048
SEQ = 8192
CTX_LEN = 256
T_ALL = SEQ + CTX_LEN
DEPTH = 4
GRID_W = 64
D_FOURIER = 512
N_FGROUPS = 4
FG_DIM = 128
D_SSD = 1536
SSD_HEAD_DIM = 64
SSD_HEADS = 24
SSD_GROUPS = 4
HEADS_PER_GROUP = 6
PAIRS_PER_GROUP = HEADS_PER_GROUP // 2
GROUP_DIM = D_SSD // SSD_GROUPS
D_STATE = 128
D_CONV = 5
CONV_DIM = 2560
CHUNK = 128
N_CHUNKS = T_ALL // CHUNK
N_LAT_CHUNKS = SEQ // CHUNK
STEP_CHUNKS = CTX_LEN // CHUNK
N_SSD_STEPS = N_CHUNKS // STEP_CHUNKS
D_MAIN = D_FOURIER + D_SSD + CONV_DIM
D_IN_PROJ = D_MAIN + 2 * SSD_HEADS
DT_PAD = 128
D_FF = 4 * D_MODEL
EPS = 1e-6
LOG2_E = float(np.log2(np.e))

SUBLANES = 8
LANES = 128
COL_TILE = 512
MLP_ROWS = 768
PROJ_ROWS = 1408
OUT_ROWS = 528
ADA_ROWS = 16
ROW_CHUNK = 128
PREP_CHUNKS = 6
FFT_N1 = 64
FFT_N2 = 128
VMEM_LIMIT = 56 * 1024 * 1024

N_PROJ_TILES = D_MAIN // COL_TILE
N_PROJ_ROW_BLOCKS = T_ALL // PROJ_ROWS
N_PROJ_STEPS = N_PROJ_ROW_BLOCKS * N_PROJ_TILES
FIRST_XBC_TILE = (D_FOURIER + D_SSD) // COL_TILE
Z_COL = 0
X_COL = D_SSD
BC_COL = 2 * D_SSD
F_COL = D_SSD + CONV_DIM


def _silu(v):
    return v * jax.nn.sigmoid(v)


def _is_ctx_rows(block_idx, rows_per_block):
    row = block_idx * rows_per_block + lax.broadcasted_iota(jnp.int32, (rows_per_block, 1), 0)
    return row >= SEQ


def _pick(is_ctx, ref):
    return jnp.where(is_ctx, ref[1:2, :], ref[0:1, :])


def _norm_modulate_rows(x_ref, g_ref, sh_ref, sc_ref, xn_scr, block_idx, rows_per_block):
    g = g_ref[0]

    def body(c, carry):
        r0 = pl.multiple_of(c * ROW_CHUNK, ROW_CHUNK)
        is_ctx = block_idx * rows_per_block + r0 >= SEQ
        x = x_ref[pl.ds(r0, ROW_CHUNK), :]
        y = x * lax.rsqrt(jnp.mean(x * x, axis=-1, keepdims=True) + EPS) * g
        y = y * (1.0 + _pick(is_ctx, sc_ref)) + _pick(is_ctx, sh_ref)
        xn_scr[pl.ds(r0, ROW_CHUNK), :] = y.astype(BF16)
        return carry

    lax.fori_loop(0, rows_per_block // ROW_CHUNK, body, 0)


def _ada_kernel(c_ref, w_ref, b_ref, o_ref):
    s = _silu(c_ref[...]).astype(BF16)
    o_ref[0] = jnp.dot(s, w_ref[0].astype(BF16), preferred_element_type=F32) + b_ref[0]


def _ada_call(cc, w_ada, b_ada):
    tn = 1024
    return pl.pallas_call(
        _ada_kernel,
        grid=(DEPTH, 6 * D_MODEL // tn),
        in_specs=[
            pl.BlockSpec((ADA_ROWS, D_MODEL), lambda l, j: (0, 0)),
            pl.BlockSpec((1, D_MODEL, tn), lambda l, j: (l, 0, j)),
            pl.BlockSpec((1, 1, tn), lambda l, j: (l, 0, j)),
        ],
        out_specs=pl.BlockSpec((1, ADA_ROWS, tn), lambda l, j: (l, 0, j)),
        out_shape=jax.ShapeDtypeStruct((DEPTH, ADA_ROWS, 6 * D_MODEL), F32),
        compiler_params=pltpu.CompilerParams(
            dimension_semantics=("arbitrary", "arbitrary"), vmem_limit_bytes=VMEM_LIMIT),
        name="ada_mod",
    )(cc, w_ada, b_ada.reshape(DEPTH, 1, 6 * D_MODEL))


def _conv_silu(v, w_ref, b_ref, group_rows, col_slice):
    rows, cols = v.shape
    tiles = group_rows // SUBLANES
    v4 = v.reshape(rows // group_rows, tiles, SUBLANES, cols)
    sub = lax.broadcasted_iota(jnp.int32, (1, 1, SUBLANES, cols), 2)
    zero_tile = jnp.zeros((rows // group_rows, 1, SUBLANES, cols), F32)
    half = D_CONV // 2
    w_ref = w_ref.at[:, :, col_slice]
    b_ref = b_ref.at[:, :, col_slice]
    acc = v4 * w_ref[0, half:half + 1, :]
    for k in range(D_CONV):
        off = k - half
        if off == 0:
            continue
        r = pltpu.roll(v4, (-off) % SUBLANES, axis=2)
        if off < 0:
            nbr = jnp.concatenate([zero_tile, r[:, :-1]], axis=1)
            shifted = jnp.where(sub < -off, nbr, r)
        else:
            nbr = jnp.concatenate([r[:, 1:], zero_tile], axis=1)
            shifted = jnp.where(sub < SUBLANES - off, r, nbr)
        acc = acc + shifted * w_ref[0, k:k + 1, :]
    return _silu(acc + b_ref[0]).reshape(rows, cols)


def _inproj_kernel(x_ref, g_ref, sh_ref, sc_ref, w_ref, wdt_ref, cw_ref, cb_ref, p_ref, dt_ref, xn_scr):
    i = pl.program_id(0)
    j = pl.program_id(1)
    last = pl.num_programs(0) - 1

    @pl.when(j == 0)
    def _():
        _norm_modulate_rows(x_ref, g_ref, sh_ref, sc_ref, xn_scr, i, PROJ_ROWS)
        lane = lax.broadcasted_iota(jnp.int32, (1, DT_PAD), 1)
        wdt = jnp.where(lane < 2 * SSD_HEADS, wdt_ref[0], jnp.zeros((), BF16))
        dt_ref[...] = jnp.dot(xn_scr[...], wdt, preferred_element_type=F32)

    def matmul():
        return jnp.dot(xn_scr[...], w_ref[0], preferred_element_type=F32)

    all_cols = slice(0, COL_TILE)

    @pl.when(j < FIRST_XBC_TILE)
    def _():
        p_ref[...] = matmul()

    @pl.when(jnp.logical_and(j >= FIRST_XBC_TILE, i < last))
    def _():
        bounds = (0, 384, 768, 1088, PROJ_ROWS)
        for r0, r1 in zip(bounds[:-1], bounds[1:]):
            acc = jnp.dot(xn_scr[r0:r1, :], w_ref[0], preferred_element_type=F32)
            p_ref[r0:r1, :] = _conv_silu(acc, cw_ref, cb_ref, GRID_W, all_cols)

    @pl.when(jnp.logical_and(j >= FIRST_XBC_TILE, i == last))
    def _():
        n_lat = PROJ_ROWS - CTX_LEN
        acc = matmul()
        p_ref[:n_lat, :] = _conv_silu(acc[:n_lat], cw_ref, cb_ref, GRID_W, all_cols)
        p_ref[n_lat:, :] = _conv_silu(acc[n_lat:], cw_ref, cb_ref, CTX_LEN, all_cols)


def _proj_col_block(j):
    return jnp.where(j == 0, N_PROJ_TILES - 1, j - 1)


def _inproj_call(l, xs, g_mix, mod, w_in, conv_w, conv_b):
    xbc_tile = lambda j: jnp.maximum(j - FIRST_XBC_TILE, 0)
    return pl.pallas_call(
        _inproj_kernel,
        grid=(N_PROJ_ROW_BLOCKS, N_PROJ_TILES),
        in_specs=[
            pl.BlockSpec((PROJ_ROWS, D_MODEL), lambda i, j: (i, 0)),
            pl.BlockSpec((1, 1, D_MODEL), lambda i, j: (l, 0, 0)),
            pl.BlockSpec((8, D_MODEL), lambda i, j: (0, 0)),
            pl.BlockSpec((8, D_MODEL), lambda i, j: (0, 1)),
            pl.BlockSpec((1, D_MODEL, COL_TILE), lambda i, j: (l, 0, j)),
            pl.BlockSpec((1, D_MODEL, DT_PAD), lambda i, j: (l, 0, D_MAIN // DT_PAD)),
            pl.BlockSpec((1, D_CONV, COL_TILE), lambda i, j: (l, 0, xbc_tile(j))),
            pl.BlockSpec((1, 1, COL_TILE), lambda i, j: (l, 0, xbc_tile(j))),
        ],
        out_specs=[
            pl.BlockSpec((PROJ_ROWS, COL_TILE), lambda i, j: (i, _proj_col_block(j))),
            pl.BlockSpec((PROJ_ROWS, DT_PAD), lambda i, j: (i, 0)),
        ],
        out_shape=[
            jax.ShapeDtypeStruct((T_ALL, D_MAIN), F32),
            jax.ShapeDtypeStruct((T_ALL, DT_PAD), F32),
        ],
        scratch_shapes=[pltpu.VMEM((PROJ_ROWS, D_MODEL), BF16)],
        compiler_params=pltpu.CompilerParams(
            dimension_semantics=("arbitrary", "arbitrary"), vmem_limit_bytes=VMEM_LIMIT),
        name="in_proj",
    )(xs, g_mix, mod, mod, w_in, w_in, conv_w, conv_b)


def _dft_tables():
    L, n1, n2 = SEQ, FFT_N1, FFT_N2
    a = np.arange(n1)[:, None, None]
    k2 = np.arange(n2)[None, :, None]
    b = np.arange(n2)[None, None, :]
    ang = 2.0 * np.pi * ((k2 * (a + n1 * b)) % L) / L
    t1 = np.concatenate([np.cos(ang), -np.sin(ang)], axis=1)
    k1 = np.arange(n1)[:, None]
    aa = np.arange(n1)[None, :]
    ang2 = 2.0 * np.pi * ((k1 * aa) % n1) / n1
    c2, s2 = np.cos(ang2), np.sin(ang2)
    f2 = np.block([[c2, s2], [-s2, c2]])
    cc = np.arange(FG_DIM)
    angc = 2.0 * np.pi * ((cc[:, None] * cc[None, :]) % FG_DIM) / FG_DIM
    lc = np.arange(CTX_LEN)
    angl = 2.0 * np.pi * ((lc[:, None] * lc[None, :]) % CTX_LEN) / CTX_LEN
    as32 = lambda v: jnp.asarray(v, dtype=F32)
    as16 = lambda v: jnp.asarray(v, dtype=F32).astype(BF16)
    return dict(t1=as16(t1), f2=as16(f2), cc=as32(np.cos(angc)), sc=as32(np.sin(angc)),
                cl=as32(np.cos(angl)), sl=as32(np.sin(angl)))


def _fourier_lat_kernel(u_ref, t1_ref, f2_ref, cc_ref, sc_ref, w_ref, o_ref,
                        zr_scr, zi_scr, xr_scr, xi_scr):
    n1, n2 = FFT_N1, FFT_N2
    scale = 1.0 / np.sqrt(float(SEQ) * FG_DIM)
    w = w_ref[0, 0]
    g1 = jnp.dot(cc_ref[...], w, precision=HIGHEST, preferred_element_type=F32) * scale
    g2 = jnp.dot(sc_ref[...], w, precision=HIGHEST, preferred_element_type=F32) * scale
    gmat = jnp.concatenate([g1, g2], axis=0).astype(BF16)

    def stage1(a, carry):
        xa = u_ref[pl.ds(a, n2, stride=n1), :].astype(BF16)
        z = jnp.dot(t1_ref[a], xa, preferred_element_type=F32)
        row0 = pl.multiple_of(a * n2, n2)
        zr_scr[pl.ds(row0, n2), :] = z[:n2]
        zi_scr[pl.ds(row0, n2), :] = z[n2:]
        return carry

    lax.fori_loop(0, n1, stage1, 0, unroll=8)

    batch = 4

    def stage2(kb, carry):
        k2 = kb * batch
        cols = []
        for q in range(batch):
            zr = zr_scr[pl.ds(k2 + q, n1, stride=n2), :]
            zi = zi_scr[pl.ds(k2 + q, n1, stride=n2), :]
            cols.append(jnp.concatenate([zr, zi], axis=0).astype(BF16))
        rhs = jnp.concatenate(cols, axis=1)
        res = jnp.dot(f2_ref[...], rhs, preferred_element_type=F32)
        for q in range(batch):
            blk = res[:, q * FG_DIM:(q + 1) * FG_DIM]
            row0 = pl.multiple_of((k2 + q) * n1, n1)
            xr_scr[pl.ds(row0, n1), :] = blk[:n1]
            xi_scr[pl.ds(row0, n1), :] = blk[n1:]
        return carry

    lax.fori_loop(0, n2 // batch, stage2, 0, unroll=4)

    def finish(k1, carry):
        xr = xr_scr[pl.ds(k1, n2, stride=n1), :]
        xi = xi_scr[pl.ds(k1, n2, stride=n1), :]
        xri = jnp.concatenate([xr, xi], axis=1).astype(BF16)
        o = jnp.dot(xri, gmat, preferred_element_type=F32)
        o_ref[pl.ds(pl.multiple_of(k1 * n2, n2), n2), :] = o.astype(o_ref.dtype)
        return carry

    lax.fori_loop(0, n1, finish, 0, unroll=8)


def _fourier_lat_call(l, p, w_f, tabs):
    col0 = F_COL // FG_DIM
    const = lambda shape: pl.BlockSpec(shape, lambda g: (0,) * len(shape))
    return pl.pallas_call(
        _fourier_lat_kernel,
        grid=(N_FGROUPS,),
        in_specs=[
            pl.BlockSpec((SEQ, FG_DIM), lambda g: (0, col0 + g)),
            const((FFT_N1, 2 * FFT_N2, FFT_N2)),
            const((2 * FFT_N1, 2 * FFT_N1)),
            const((FG_DIM, FG_DIM)),
            const((FG_DIM, FG_DIM)),
            pl.BlockSpec((1, 1, FG_DIM, FG_DIM), lambda g: (l, g, 0, 0)),
        ],
        out_specs=pl.BlockSpec((SEQ, FG_DIM), lambda g: (0, g)),
        out_shape=jax.ShapeDtypeStruct((T_ALL, D_FOURIER), BF16),
        scratch_shapes=[pltpu.VMEM((SEQ, FG_DIM), F32)] * 4,
        compiler_params=pltpu.CompilerParams(
            dimension_semantics=("arbitrary",), vmem_limit_bytes=VMEM_LIMIT),
        name="fourier_lat",
    )(p, tabs["t1"], tabs["f2"], tabs["cc"], tabs["sc"], w_f)


def _fourier_ctx_kernel(u_ref, cl_ref, sl_ref, cc_ref, sc_ref, w_ref, f_hbm_ref, o_ref):
    del f_hbm_ref
    scale = 1.0 / np.sqrt(float(CTX_LEN) * FG_DIM)
    dot = functools.partial(jnp.dot, precision=HIGHEST, preferred_element_type=F32)
    w = w_ref[0, 0]
    u = u_ref[...]
    a = dot(u, dot(cc_ref[...], w))
    b = dot(u, dot(sc_ref[...], w))
    o_ref[...] = ((dot(cl_ref[...], a) - dot(sl_ref[...], b)) * scale).astype(o_ref.dtype)


def _fourier_ctx_call(l, p, w_f, tabs, f):
    col0 = F_COL // FG_DIM
    row_blk = SEQ // CTX_LEN
    const = lambda shape: pl.BlockSpec(shape, lambda g: (0,) * len(shape))
    return pl.pallas_call(
        _fourier_ctx_kernel,
        grid=(N_FGROUPS,),
        in_specs=[
            pl.BlockSpec((CTX_LEN, FG_DIM), lambda g: (row_blk, col0 + g)),
            const((CTX_LEN, CTX_LEN)),
            const((CTX_LEN, CTX_LEN)),
            const((FG_DIM, FG_DIM)),
            const((FG_DIM, FG_DIM)),
            pl.BlockSpec((1, 1, FG_DIM, FG_DIM), lambda g: (l, g, 0, 0)),
            pl.BlockSpec(memory_space=pl.ANY),
        ],
        out_specs=pl.BlockSpec((CTX_LEN, FG_DIM), lambda g: (row_blk, g)),
        out_shape=jax.ShapeDtypeStruct((T_ALL, D_FOURIER), BF16),
        input_output_aliases={6: 0},
        compiler_params=pltpu.CompilerParams(
            dimension_semantics=("arbitrary",), vmem_limit_bytes=VMEM_LIMIT),
        name="fourier_ctx",
    )(p, tabs["cl"], tabs["sl"], tabs["cc"], tabs["sc"], w_f, f)


def _ssd_prep_kernel(dtr_ref, dtb_ref, alog_ref, tri_ref, cum_ref, tr_ref):
    q = CHUNK
    is_fwd = lax.broadcasted_iota(jnp.int32, (1, DT_PAD), 1) < SSD_HEADS
    neg_a = -jnp.exp(alog_ref[0])
    for c in range(PREP_CHUNKS):
        rs = slice(c * q, (c + 1) * q)
        v = dtr_ref[rs, :] + dtb_ref[0]
        dt = jnp.maximum(v, 0.0) + jnp.log1p(jnp.exp(-jnp.abs(v)))
        dta = dt * neg_a
        run = jnp.dot(tri_ref[0], dta, precision=HIGHEST, preferred_element_type=F32)
        rev = jnp.dot(tri_ref[1], dta, precision=HIGHEST, preferred_element_type=F32)
        cum = jnp.where(is_fwd, run, rev)
        total = jnp.where(is_fwd, run[q - 1:q, :], rev[0:1, :])
        cum2 = cum * LOG2_E
        cum_ref[rs, :] = cum2
        tr_ref[c, 0] = (cum2 - jnp.log2(dt)).T
        tr_ref[c, 1] = (dt * jnp.exp(total - cum)).T


def _ssd_prep_call(l, dtr, dtb, alog, tri):
    layer = lambda width: pl.BlockSpec((1, 1, width), lambda t: (l, 0, 0))
    rows = PREP_CHUNKS * CHUNK
    return pl.pallas_call(
        _ssd_prep_kernel,
        grid=(N_CHUNKS // PREP_CHUNKS,),
        in_specs=[pl.BlockSpec((rows, DT_PAD), lambda t: (t, 0)), layer(DT_PAD), layer(DT_PAD),
                  pl.BlockSpec((2, CHUNK, CHUNK), lambda t: (0, 0, 0))],
        out_specs=[pl.BlockSpec((rows, DT_PAD), lambda t: (t, 0)),
                   pl.BlockSpec((PREP_CHUNKS, 2, DT_PAD, CHUNK), lambda t: (t, 0, 0, 0))],
        out_shape=[jax.ShapeDtypeStruct((T_ALL, DT_PAD), F32),
                   jax.ShapeDtypeStruct((N_CHUNKS, 2, DT_PAD, CHUNK), F32)],
        compiler_params=pltpu.CompilerParams(
            dimension_semantics=("arbitrary",), vmem_limit_bytes=VMEM_LIMIT),
        name="ssd_prep",
    )(dtr, dtb, alog, tri)


def _ssd_kernel(direction, *refs):
    forward = direction == 0
    if forward:
        x_ref, bc_ref, cum_ref, tr_ref, wsrc_ref, o_ref, wdst_ref, h_scr = refs
    else:
        (x_ref, bc_ref, cum_ref, tr_ref, yf_ref, z_ref, dsk_ref, gn_ref, wsrc_ref,
         o_ref, wdst_ref, h_scr) = refs
    wdst_ref[...] = wsrc_ref[0].astype(BF16)
    q = CHUNK
    edge = q - 1 if forward else 0

    @pl.when(pl.program_id(0) == 0)
    def _():
        h_scr[...] = jnp.zeros_like(h_scr)

    row = lax.broadcasted_iota(jnp.int32, (q, q), 0)
    col = lax.broadcasted_iota(jnp.int32, (q, q), 1)
    keep = (col <= row) if forward else (col >= row)
    low_lanes = col < SSD_HEAD_DIM

    for c in (range(STEP_CHUNKS) if forward else reversed(range(STEP_CHUNKS))):
        rs = slice(c * q, (c + 1) * q)
        cum2 = cum_ref[rs, :]
        cdt_t = tr_ref[c, 0]
        w_t = tr_ref[c, 1]
        x = x_ref[rs, :]
        x16 = x.astype(BF16)

        ys = []
        for g in range(SSD_GROUPS):
            bm = bc_ref[rs, g * D_STATE:(g + 1) * D_STATE]
            cm = bc_ref[rs, (SSD_GROUPS + g) * D_STATE:(SSD_GROUPS + g + 1) * D_STATE]
            bt = bm.T
            c16 = cm.astype(BF16)
            cb = jnp.dot(c16, bt.astype(BF16), preferred_element_type=F32)
            h_prev = [h_scr[PAIRS_PER_GROUP * g + j] for j in range(PAIRS_PER_GROUP)]
            y_off = jnp.dot(c16, jnp.concatenate(h_prev, axis=1).astype(BF16),
                            preferred_element_type=F32)
            for j in range(PAIRS_PER_GROUP):
                top, bot, ecol = [], [], []
                for r in range(2):
                    hh = SSD_HEADS * direction + HEADS_PER_GROUP * g + 2 * j + r
                    colb = jnp.broadcast_to(cum2[:, hh:hh + 1], (q, q))
                    seg = colb - cdt_t[hh:hh + 1, :]
                    decay_dt = jnp.exp2(jnp.where(keep, seg, -jnp.inf))
                    top.append((decay_dt * cb).astype(BF16))
                    bot.append((bt * w_t[hh:hh + 1, :]).astype(BF16))
                    ecol.append(jnp.exp2(colb))
                lhs = jnp.concatenate([jnp.concatenate(top, axis=1),
                                       jnp.concatenate(bot, axis=1)], axis=0)
                c0 = g * GROUP_DIM + j * LANES
                xp = x16[:, c0:c0 + LANES]
                zero = jnp.zeros_like(xp)
                rhs = jnp.concatenate([jnp.where(low_lanes, xp, zero),
                                       jnp.where(low_lanes, zero, xp)], axis=0)
                res = jnp.dot(lhs, rhs, preferred_element_type=F32)
                escale = jnp.where(low_lanes, ecol[0], ecol[1])
                ys.append(res[:q] + y_off[:, j * LANES:(j + 1) * LANES] * escale)
                h_scr[PAIRS_PER_GROUP * g + j] = h_prev[j] * escale[edge:edge + 1, :] + res[q:]
        y = jnp.concatenate(ys, axis=1)

        if forward:
            o_ref[rs, :] = y
        else:
            y = yf_ref[rs, :] + y + dsk_ref[0] * x
            u = y * _silu(z_ref[rs, :])
            outs = []
            for g in range(SSD_GROUPS):
                ug = u[:, g * GROUP_DIM:(g + 1) * GROUP_DIM]
                outs.append(ug * lax.rsqrt(jnp.mean(ug * ug, axis=-1, keepdims=True) + EPS))
            o_ref[rs, :] = (jnp.concatenate(outs, axis=1) * gn_ref[0]).astype(o_ref.dtype)


def _fwd_block(t):
    return jnp.where(t == 0, N_SSD_STEPS - 1, t - 1)


def _bwd_block(t):
    return N_SSD_STEPS - 1 - t


def _ssd_tri():
    lower = np.tril(np.ones((CHUNK, CHUNK), np.float32))
    return jnp.asarray(np.stack([lower, lower.T]))


def _ssd_call(l, direction, p, cum, tr, w_mlp, extra=None):
    block_of = _fwd_block if direction == 0 else _bwd_block
    w_rows, w_cols = w_mlp.shape[1:]
    cast_steps = N_SSD_STEPS - 1
    slab = w_rows // cast_steps
    slab_of = lambda t: jnp.minimum(t, cast_steps - 1)
    rows = lambda width, cblk: pl.BlockSpec((STEP_CHUNKS * CHUNK, width), lambda t: (block_of(t), cblk))
    layer = lambda width: pl.BlockSpec((1, 1, width), lambda t: (l, 0, 0))
    bc_width = 2 * SSD_GROUPS * D_STATE
    in_specs = [rows(D_SSD, X_COL // D_SSD), rows(bc_width, BC_COL // bc_width), rows(DT_PAD, 0),
                pl.BlockSpec((STEP_CHUNKS, 2, DT_PAD, CHUNK), lambda t: (block_of(t), 0, 0, 0))]
    args = [p, p, cum, tr]
    if direction == 0:
        out_dtype = F32
    else:
        yf, dsk, gn = extra
        in_specs += [rows(D_SSD, 0), rows(D_SSD, Z_COL // D_SSD), layer(D_SSD), layer(D_SSD)]
        args += [yf, p, dsk, gn]
        out_dtype = BF16
    in_specs.append(pl.BlockSpec((1, slab, w_cols), lambda t: (l, slab_of(t), 0)))
    args.append(w_mlp)
    return pl.pallas_call(
        functools.partial(_ssd_kernel, direction),
        grid=(N_SSD_STEPS,),
        in_specs=in_specs,
        out_specs=[rows(D_SSD, 0), pl.BlockSpec((slab, w_cols), lambda t: (slab_of(t), 0))],
        out_shape=[jax.ShapeDtypeStruct((T_ALL, D_SSD), out_dtype),
                   jax.ShapeDtypeStruct((w_rows, w_cols), BF16)],
        scratch_shapes=[pltpu.VMEM((SSD_GROUPS * PAIRS_PER_GROUP, D_STATE, LANES), F32)],
        compiler_params=pltpu.CompilerParams(
            dimension_semantics=("arbitrary",), vmem_limit_bytes=VMEM_LIMIT),
        name="ssd_fwd" if direction == 0 else "ssd_bwd",
    )(*args)


def _outproj_kernel(x_ref, f_ref, y_ref, w_ref, gt_ref, o_ref, w16_scr):
    i = pl.program_id(0)

    @pl.when(i == 0)
    def _():
        rows = 256

        def body(c, carry):
            rs = pl.ds(pl.multiple_of(c * rows, rows), rows)
            w16_scr[rs, :] = w_ref[0, rs, :].astype(BF16)
            return carry

        lax.fori_loop(0, D_MODEL // rows, body, 0)

    acc = (jnp.dot(f_ref[...], w16_scr[:D_FOURIER, :], preferred_element_type=F32)
           + jnp.dot(y_ref[...], w16_scr[D_FOURIER:, :], preferred_element_type=F32))
    gate = _pick(_is_ctx_rows(i, OUT_ROWS), gt_ref)
    o_ref[...] = x_ref[...] + gate * acc


def _outproj_call(l, xs, f, y, w_out, mod):
    rows = lambda width: pl.BlockSpec((OUT_ROWS, width), lambda i: (i, 0))
    return pl.pallas_call(
        _outproj_kernel,
        grid=(T_ALL // OUT_ROWS,),
        in_specs=[
            rows(D_MODEL), rows(D_FOURIER), rows(D_SSD),
            pl.BlockSpec((1, D_MODEL, D_MODEL), lambda i: (l, 0, 0), pipeline_mode=pl.Buffered(1)),
            pl.BlockSpec((8, D_MODEL), lambda i: (0, 2)),
        ],
        out_specs=rows(D_MODEL),
        out_shape=jax.ShapeDtypeStruct((T_ALL, D_MODEL), F32),
        scratch_shapes=[pltpu.VMEM((D_MODEL, D_MODEL), BF16)],
        compiler_params=pltpu.CompilerParams(
            dimension_semantics=("arbitrary",), vmem_limit_bytes=VMEM_LIMIT),
        name="out_proj",
    )(xs, f, y, w_out, mod)


def _mlp_kernel(final, x_ref, g_ref, sh_ref, sc_ref, gt_ref, w1_ref, w2_ref, gf_ref, o_ref, xn_scr, h_scr):
    i = pl.program_id(0)
    j = pl.program_id(1)
    n_ff = pl.num_programs(1) - 1

    def hidden():
        h = jnp.dot(xn_scr[...], w1_ref[...], preferred_element_type=F32)
        return jnp.square(jnp.maximum(h, 0.0)).astype(BF16)

    @pl.when(j == 0)
    def _():
        _norm_modulate_rows(x_ref, g_ref, sh_ref, sc_ref, xn_scr, i, MLP_ROWS)
        o_ref[...] = jnp.zeros_like(o_ref)
        h_scr[0] = hidden()

    for parity in (0, 1):
        @pl.when(jnp.logical_and(jnp.logical_and(j > 0, j < n_ff), j % 2 == parity))
        def _():
            h_scr[parity] = hidden()
            o_ref[...] += jnp.dot(h_scr[1 - parity], w2_ref[...], preferred_element_type=F32)

    @pl.when(j == n_ff)
    def _():
        acc = o_ref[...] + jnp.dot(h_scr[(D_FF // COL_TILE - 1) % 2], w2_ref[...], preferred_element_type=F32)
        gate = _pick(_is_ctx_rows(i, MLP_ROWS), gt_ref)
        y = x_ref[...] + gate * acc
        if final:
            y = y * lax.rsqrt(jnp.mean(y * y, axis=-1, keepdims=True) + EPS) * gf_ref[...]
        o_ref[...] = y


def _mlp_call(l, xs, g_mlp, mod, w1, w2, g_final, final):
    n_ff = D_FF // COL_TILE
    return pl.pallas_call(
        functools.partial(_mlp_kernel, final),
        grid=(T_ALL // MLP_ROWS, n_ff + 1),
        in_specs=[
            pl.BlockSpec((MLP_ROWS, D_MODEL), lambda i, j: (i, 0)),
            pl.BlockSpec((1, 1, D_MODEL), lambda i, j: (l, 0, 0)),
            pl.BlockSpec((8, D_MODEL), lambda i, j: (0, 3)),
            pl.BlockSpec((8, D_MODEL), lambda i, j: (0, 4)),
            pl.BlockSpec((8, D_MODEL), lambda i, j: (0, 5)),
            pl.BlockSpec((D_MODEL, COL_TILE), lambda i, j: (0, jnp.minimum(j, n_ff - 1))),
            pl.BlockSpec((COL_TILE, D_MODEL), lambda i, j: (jnp.maximum(j - 1, 0), 0)),
            pl.BlockSpec((1, D_MODEL), lambda i, j: (0, 0)),
        ],
        out_specs=pl.BlockSpec((MLP_ROWS, D_MODEL), lambda i, j: (i, 0)),
        out_shape=jax.ShapeDtypeStruct((SEQ if final else T_ALL, D_MODEL), F32),
        scratch_shapes=[pltpu.VMEM((MLP_ROWS, D_MODEL), BF16),
                        pltpu.VMEM((2, MLP_ROWS, COL_TILE), BF16)],
        compiler_params=pltpu.CompilerParams(
            dimension_semantics=("arbitrary", "arbitrary"), vmem_limit_bytes=VMEM_LIMIT),
        name="mlp",
    )(xs, g_mlp, mod, mod, mod, w1, w2, g_final)


def _mixer_layer(l, xs, mod, tabs, tri, g_mix, w_in, conv_w, conv_b, dtb, alog, dsk, g_ssd_norm,
                 w_fourier, w_out, w_mlp1, w_mlp2):
    p, dtr = _inproj_call(l, xs, g_mix, mod, w_in, conv_w, conv_b)
    f = _fourier_lat_call(l, p, w_fourier, tabs)
    f = _fourier_ctx_call(l, p, w_fourier, tabs, f)
    cum, tr = _ssd_prep_call(l, dtr, dtb, alog, tri)
    yf, w1 = _ssd_call(l, 0, p, cum, tr, w_mlp1)
    y, w2 = _ssd_call(l, 1, p, cum, tr, w_mlp2, extra=(yf, dsk, g_ssd_norm))
    return _outproj_call(l, xs, f, y, w_out, mod), w1, w2


def kernel(x, c, ctx, c_ctx, w_ada, b_ada, g_mix, w_in, conv_w, conv_b, dt_bias, a_log, d_skip,
           g_ssd_norm, w_fourier, w_out, g_mlp, w_mlp1, w_mlp2, g_final):
    assert x.shape == (1, SEQ, D_MODEL) and ctx.shape == (1, CTX_LEN, D_MODEL)
    tabs = _dft_tables()
    tri = _ssd_tri()

    xs = jnp.concatenate([x[0], ctx[0]], axis=0)
    cc = jnp.concatenate([c, c_ctx[None, :], jnp.zeros((ADA_ROWS - 2, D_MODEL), F32)], axis=0)
    mods = _ada_call(cc, w_ada, b_ada)

    pad_heads = ((0, 0), (0, 0), (0, DT_PAD - 2 * SSD_HEADS))
    dtb = jnp.pad(dt_bias.reshape(DEPTH, 1, 2 * SSD_HEADS), pad_heads)
    alog = jnp.pad(a_log.reshape(DEPTH, 1, 2 * SSD_HEADS), pad_heads)
    dsk = jnp.repeat(d_skip, SSD_HEAD_DIM, axis=1).reshape(DEPTH, 1, D_SSD)
    row3 = lambda a: a.reshape(DEPTH, 1, a.shape[-1])
    w_in16 = w_in.astype(BF16)

    for l in range(DEPTH):
        xs, w1, w2 = _mixer_layer(l, xs, mods[l], tabs, tri, row3(g_mix), w_in16, conv_w, row3(conv_b),
                                  dtb, alog, dsk, row3(g_ssd_norm), w_fourier, w_out, w_mlp1, w_mlp2)
        xs = _mlp_call(l, xs, row3(g_mlp), mods[l], w1, w2, g_final[None, :], final=l == DEPTH - 1)

    return xs[None]
```

```python
import functools

import numpy as np
import jax
import jax.numpy as jnp
from jax import lax
from jax.experimental import pallas as pl
from jax.experimental.pallas import tpu as pltpu

F32 = jnp.float32
BF16 = jnp.bfloat16
HIGHEST = lax.Precision.HIGHEST

D_MODEL = 2048
SEQ = 8192
CTX_LEN = 256
T_ALL = SEQ + CTX_LEN
DEPTH = 4
GRID_W = 64
D_FOURIER = 512
N_FGROUPS = 4
FG_DIM = 128
D_SSD = 1536
SSD_HEAD_DIM = 64
SSD_HEADS = 24
SSD_GROUPS = 4
HEADS_PER_GROUP = 6
PAIRS_PER_GROUP = HEADS_PER_GROUP // 2
GROUP_DIM = D_SSD // SSD_GROUPS
D_STATE = 128
D_CONV = 5
CONV_DIM = 2560
CHUNK = 128
N_CHUNKS = T_ALL // CHUNK
N_LAT_CHUNKS = SEQ // CHUNK
STEP_CHUNKS = CTX_LEN // CHUNK
N_SSD_STEPS = N_CHUNKS // STEP_CHUNKS
D_MAIN = D_FOURIER + D_SSD + CONV_DIM
D_IN_PROJ = D_MAIN + 2 * SSD_HEADS
DT_PAD = 128
D_FF = 4 * D_MODEL
EPS = 1e-6
LOG2_E = float(np.log2(np.e))

SUBLANES = 8
LANES = 128
COL_TILE = 512
MLP_ROWS = 768
PROJ_ROWS = 1408
OUT_ROWS = 528
ADA_ROWS = 16
ROW_CHUNK = 128
PREP_CHUNKS = 6
FFT_N1 = 64
FFT_N2 = 128
VMEM_LIMIT = 56 * 1024 * 1024

N_PROJ_TILES = D_MAIN // COL_TILE
N_PROJ_ROW_BLOCKS = T_ALL // PROJ_ROWS
N_PROJ_STEPS = N_PROJ_ROW_BLOCKS * N_PROJ_TILES
FIRST_XBC_TILE = (D_FOURIER + D_SSD) // COL_TILE
Z_COL = 0
X_COL = D_SSD
BC_COL = 2 * D_SSD
F_COL = D_SSD + CONV_DIM


def _silu(v):
    return v * jax.nn.sigmoid(v)


def _is_ctx_rows(block_idx, rows_per_block):
    row = block_idx * rows_per_block + lax.broadcasted_iota(jnp.int32, (rows_per_block, 1), 0)
    return row >= SEQ


def _pick(is_ctx, ref):
    return jnp.where(is_ctx, ref[1:2, :], ref[0:1, :])


def _norm_modulate_rows(x_ref, g_ref, sh_ref, sc_ref, xn_scr, block_idx, rows_per_block):
    g = g_ref[0]

    def body(c, carry):
        r0 = pl.multiple_of(c * ROW_CHUNK, ROW_CHUNK)
        is_ctx = block_idx * rows_per_block + r0 >= SEQ
        x = x_ref[pl.ds(r0, ROW_CHUNK), :]
        y = x * lax.rsqrt(jnp.mean(x * x, axis=-1, keepdims=True) + EPS) * g
        y = y * (1.0 + _pick(is_ctx, sc_ref)) + _pick(is_ctx, sh_ref)
        xn_scr[pl.ds(r0, ROW_CHUNK), :] = y.astype(BF16)
        return carry

    lax.fori_loop(0, rows_per_block // ROW_CHUNK, body, 0)


def _ada_kernel(c_ref, w_ref, b_ref, o_ref):
    s = _silu(c_ref[...]).astype(BF16)
    o_ref[0] = jnp.dot(s, w_ref[0].astype(BF16), preferred_element_type=F32) + b_ref[0]


def _ada_call(cc, w_ada, b_ada):
    tn = 1024
    return pl.pallas_call(
        _ada_kernel,
        grid=(DEPTH, 6 * D_MODEL // tn),
        in_specs=[
            pl.BlockSpec((ADA_ROWS, D_MODEL), lambda l, j: (0, 0)),
            pl.BlockSpec((1, D_MODEL, tn), lambda l, j: (l, 0, j)),
            pl.BlockSpec((1, 1, tn), lambda l, j: (l, 0, j)),
        ],
        out_specs=pl.BlockSpec((1, ADA_ROWS, tn), lambda l, j: (l, 0, j)),
        out_shape=jax.ShapeDtypeStruct((DEPTH, ADA_ROWS, 6 * D_MODEL), F32),
        compiler_params=pltpu.CompilerParams(
            dimension_semantics=("arbitrary", "arbitrary"), vmem_limit_bytes=VMEM_LIMIT),
        name="ada_mod",
    )(cc, w_ada, b_ada.reshape(DEPTH, 1, 6 * D_MODEL))


def _conv_silu(v, w_ref, b_ref, group_rows, col_slice):
    rows, cols = v.shape
    tiles = group_rows // SUBLANES
    v4 = v.reshape(rows // group_rows, tiles, SUBLANES, cols)
    sub = lax.broadcasted_iota(jnp.int32, (1, 1, SUBLANES, cols), 2)
    zero_tile = jnp.zeros((rows // group_rows, 1, SUBLANES, cols), F32)
    half = D_CONV // 2
    w_ref = w_ref.at[:, :, col_slice]
    b_ref = b_ref.at[:, :, col_slice]
    acc = v4 * w_ref[0, half:half + 1, :]
    for k in range(D_CONV):
        off = k - half
        if off == 0:
            continue
        r = pltpu.roll(v4, (-off) % SUBLANES, axis=2)
        if off < 0:
            nbr = jnp.concatenate([zero_tile, r[:, :-1]], axis=1)
            shifted = jnp.where(sub < -off, nbr, r)
        else:
            nbr = jnp.concatenate([r[:, 1:], zero_tile], axis=1)
            shifted = jnp.where(sub < SUBLANES - off, r, nbr)
        acc = acc + shifted * w_ref[0, k:k + 1, :]
    return _silu(acc + b_ref[0]).reshape(rows, cols)


def _inproj_kernel(x_ref, g_ref, sh_ref, sc_ref, w_ref, wdt_ref, cw_ref, cb_ref, p_ref, dt_ref, xn_scr):
    i = pl.program_id(0)
    j = pl.program_id(1)
    last = pl.num_programs(0) - 1

    @pl.when(j == 0)
    def _():
        _norm_modulate_rows(x_ref, g_ref, sh_ref, sc_ref, xn_scr, i, PROJ_ROWS)
        lane = lax.broadcasted_iota(jnp.int32, (1, DT_PAD), 1)
        wdt = jnp.where(lane < 2 * SSD_HEADS, wdt_ref[0], jnp.zeros((), BF16))
        dt_ref[...] = jnp.dot(xn_scr[...], wdt, preferred_element_type=F32)

    def matmul():
        return jnp.dot(xn_scr[...], w_ref[0], preferred_element_type=F32)

    all_cols = slice(0, COL_TILE)

    @pl.when(j < FIRST_XBC_TILE)
    def _():
        p_ref[...] = matmul()

    @pl.when(jnp.logical_and(j >= FIRST_XBC_TILE, i < last))
    def _():
        bounds = (0, 384, 768, 1088, PROJ_ROWS)
        for r0, r1 in zip(bounds[:-1], bounds[1:]):
            acc = jnp.dot(xn_scr[r0:r1, :], w_ref[0], preferred_element_type=F32)
            p_ref[r0:r1, :] = _conv_silu(acc, cw_ref, cb_ref, GRID_W, all_cols)

    @pl.when(jnp.logical_and(j >= FIRST_XBC_TILE, i == last))
    def _():
        n_lat = PROJ_ROWS - CTX_LEN
        acc = matmul()
        p_ref[:n_lat, :] = _conv_silu(acc[:n_lat], cw_ref, cb_ref, GRID_W, all_cols)
        p_ref[n_lat:, :] = _conv_silu(acc[n_lat:], cw_ref, cb_ref, CTX_LEN, all_cols)


def _proj_col_block(j):
    return jnp.where(j == 0, N_PROJ_TILES - 1, j - 1)


def _inproj_call(l, xs, g_mix, mod, w_in, conv_w, conv_b):
    xbc_tile = lambda j: jnp.maximum(j - FIRST_XBC_TILE, 0)
    return pl.pallas_call(
        _inproj_kernel,
        grid=(N_PROJ_ROW_BLOCKS, N_PROJ_TILES),
        in_specs=[
            pl.BlockSpec((PROJ_ROWS, D_MODEL), lambda i, j: (i, 0)),
            pl.BlockSpec((1, 1, D_MODEL), lambda i, j: (l, 0, 0)),
            pl.BlockSpec((8, D_MODEL), lambda i, j: (0, 0)),
            pl.BlockSpec((8, D_MODEL), lambda i, j: (0, 1)),
            pl.BlockSpec((1, D_MODEL, COL_TILE), lambda i, j: (l, 0, j)),
            pl.BlockSpec((1, D_MODEL, DT_PAD), lambda i, j: (l, 0, D_MAIN // DT_PAD)),
            pl.BlockSpec((1, D_CONV, COL_TILE), lambda i, j: (l, 0, xbc_tile(j))),
            pl.BlockSpec((1, 1, COL_TILE), lambda i, j: (l, 0, xbc_tile(j))),
        ],
        out_specs=[
            pl.BlockSpec((PROJ_ROWS, COL_TILE), lambda i, j: (i, _proj_col_block(j))),
            pl.BlockSpec((PROJ_ROWS, DT_PAD), lambda i, j: (i, 0)),
        ],
        out_shape=[
            jax.ShapeDtypeStruct((T_ALL, D_MAIN), F32),
            jax.ShapeDtypeStruct((T_ALL, DT_PAD), F32),
        ],
        scratch_shapes=[pltpu.VMEM((PROJ_ROWS, D_MODEL), BF16)],
        compiler_params=pltpu.CompilerParams(
            dimension_semantics=("arbitrary", "arbitrary"), vmem_limit_bytes=VMEM_LIMIT),
        name="in_proj",
    )(xs, g_mix, mod, mod, w_in, w_in, conv_w, conv_b)


def _dft_tables():
    L, n1, n2 = SEQ, FFT_N1, FFT_N2
    a = np.arange(n1)[:, None, None]
    k2 = np.arange(n2)[None, :, None]
    b = np.arange(n2)[None, None, :]
    ang = 2.0 * np.pi * ((k2 * (a + n1 * b)) % L) / L
    t1 = np.concatenate([np.cos(ang), -np.sin(ang)], axis=1)
    k1 = np.arange(n1)[:, None]
    aa = np.arange(n1)[None, :]
    ang2 = 2.0 * np.pi * ((k1 * aa) % n1) / n1
    c2, s2 = np.cos(ang2), np.sin(ang2)
    f2 = np.block([[c2, s2], [-s2, c2]])
    cc = np.arange(FG_DIM)
    angc = 2.0 * np.pi * ((cc[:, None] * cc[None, :]) % FG_DIM) / FG_DIM
    lc = np.arange(CTX_LEN)
    angl = 2.0 * np.pi * ((lc[:, None] * lc[None, :]) % CTX_LEN) / CTX_LEN
    as32 = lambda v: jnp.asarray(v, dtype=F32)
    as16 = lambda v: jnp.asarray(v, dtype=F32).astype(BF16)
    return dict(t1=as16(t1), f2=as16(f2), cc=as32(np.cos(angc)), sc=as32(np.sin(angc)),
                cl=as32(np.cos(angl)), sl=as32(np.sin(angl)))


def _fourier_lat_kernel(u_ref, t1_ref, f2_ref, cc_ref, sc_ref, w_ref, o_ref,
                        zr_scr, zi_scr, xr_scr, xi_scr):
    n1, n2 = FFT_N1, FFT_N2
    scale = 1.0 / np.sqrt(float(SEQ) * FG_DIM)
    w = w_ref[0, 0]
    g1 = jnp.dot(cc_ref[...], w, precision=HIGHEST, preferred_element_type=F32) * scale
    g2 = jnp.dot(sc_ref[...], w, precision=HIGHEST, preferred_element_type=F32) * scale
    gmat = jnp.concatenate([g1, g2], axis=0).astype(BF16)

    batch = SUBLANES

    def stage1(a, carry):
        xa = u_ref[pl.ds(a, n2, stride=n1), :].astype(BF16)
        z = jnp.dot(t1_ref[a], xa, preferred_element_type=F32)
        zr_scr[a] = z[:n2]
        zi_scr[a] = z[n2:]
        return carry

    lax.fori_loop(0, n1, stage1, 0, unroll=8)

    def stage2(kb, carry):
        k2 = pl.multiple_of(kb * batch, batch)
        zr = jnp.swapaxes(zr_scr[:, pl.ds(k2, batch), :], 0, 1)
        zi = jnp.swapaxes(zi_scr[:, pl.ds(k2, batch), :], 0, 1)
        cols = [jnp.concatenate([zr[q], zi[q]], axis=0).astype(BF16) for q in range(batch)]
        rhs = jnp.concatenate(cols, axis=1)
        res = jnp.dot(f2_ref[...], rhs, preferred_element_type=F32)
        for q in range(batch):
            blk = res[:, q * FG_DIM:(q + 1) * FG_DIM]
            xr_scr[k2 + q] = blk[:n1]
            xi_scr[k2 + q] = blk[n1:]
        return carry

    lax.fori_loop(0, n2 // batch, stage2, 0, unroll=2)

    def finish(kb, carry):
        k1 = pl.multiple_of(kb * batch, batch)
        xr = jnp.swapaxes(xr_scr[:, pl.ds(k1, batch), :], 0, 1)
        xi = jnp.swapaxes(xi_scr[:, pl.ds(k1, batch), :], 0, 1)
        for q in range(batch):
            xri = jnp.concatenate([xr[q], xi[q]], axis=1).astype(BF16)
            o = jnp.dot(xri, gmat, preferred_element_type=F32)
            row0 = pl.multiple_of((k1 + q) * n2, n2)
            o_ref[pl.ds(row0, n2), :] = o.astype(o_ref.dtype)
        return carry

    lax.fori_loop(0, n1 // batch, finish, 0)


def _fourier_lat_call(l, p, w_f, tabs):
    col0 = F_COL // FG_DIM
    const = lambda shape: pl.BlockSpec(shape, lambda g: (0,) * len(shape))
    return pl.pallas_call(
        _fourier_lat_kernel,
        grid=(N_FGROUPS,),
        in_specs=[
            pl.BlockSpec((SEQ, FG_DIM), lambda g: (0, col0 + g)),
            const((FFT_N1, 2 * FFT_N2, FFT_N2)),
            const((2 * FFT_N1, 2 * FFT_N1)),
            const((FG_DIM, FG_DIM)),
            const((FG_DIM, FG_DIM)),
            pl.BlockSpec((1, 1, FG_DIM, FG_DIM), lambda g: (l, g, 0, 0)),
        ],
        out_specs=pl.BlockSpec((SEQ, FG_DIM), lambda g: (0, g)),
        out_shape=jax.ShapeDtypeStruct((T_ALL, D_FOURIER), BF16),
        scratch_shapes=([pltpu.VMEM((FFT_N1, FFT_N2, FG_DIM), F32)] * 2
                        + [pltpu.VMEM((FFT_N2, FFT_N1, FG_DIM), F32)] * 2),
        compiler_params=pltpu.CompilerParams(
            dimension_semantics=("arbitrary",), vmem_limit_bytes=VMEM_LIMIT),
        name="fourier_lat",
    )(p, tabs["t1"], tabs["f2"], tabs["cc"], tabs["sc"], w_f)


def _fourier_ctx_kernel(u_ref, cl_ref, sl_ref, cc_ref, sc_ref, w_ref, f_hbm_ref, o_ref):
    del f_hbm_ref
    scale = 1.0 / np.sqrt(float(CTX_LEN) * FG_DIM)
    dot = functools.partial(jnp.dot, precision=HIGHEST, preferred_element_type=F32)
    w = w_ref[0, 0]
    u = u_ref[...]
    a = dot(u, dot(cc_ref[...], w))
    b = dot(u, dot(sc_ref[...], w))
    o_ref[...] = ((dot(cl_ref[...], a) - dot(sl_ref[...], b)) * scale).astype(o_ref.dtype)


def _fourier_ctx_call(l, p, w_f, tabs, f):
    col0 = F_COL // FG_DIM
    row_blk = SEQ // CTX_LEN
    const = lambda shape: pl.BlockSpec(shape, lambda g: (0,) * len(shape))
    return pl.pallas_call(
        _fourier_ctx_kernel,
        grid=(N_FGROUPS,),
        in_specs=[
            pl.BlockSpec((CTX_LEN, FG_DIM), lambda g: (row_blk, col0 + g)),
            const((CTX_LEN, CTX_LEN)),
            const((CTX_LEN, CTX_LEN)),
            const((FG_DIM, FG_DIM)),
            const((FG_DIM, FG_DIM)),
            pl.BlockSpec((1, 1, FG_DIM, FG_DIM), lambda g: (l, g, 0, 0)),
            pl.BlockSpec(memory_space=pl.ANY),
        ],
        out_specs=pl.BlockSpec((CTX_LEN, FG_DIM), lambda g: (row_blk, g)),
        out_shape=jax.ShapeDtypeStruct((T_ALL, D_FOURIER), BF16),
        input_output_aliases={6: 0},
        compiler_params=pltpu.CompilerParams(
            dimension_semantics=("arbitrary",), vmem_limit_bytes=VMEM_LIMIT),
        name="fourier_ctx",
    )(p, tabs["cl"], tabs["sl"], tabs["cc"], tabs["sc"], w_f, f)


def _ssd_prep_kernel(dtr_ref, dtb_ref, alog_ref, tri_ref, cum_ref, tr_ref):
    q = CHUNK
    is_fwd = lax.broadcasted_iota(jnp.int32, (1, DT_PAD), 1) < SSD_HEADS
    neg_a = -jnp.exp(alog_ref[0])
    for c in range(PREP_CHUNKS):
        rs = slice(c * q, (c + 1) * q)
        v = dtr_ref[rs, :] + dtb_ref[0]
        dt = jnp.maximum(v, 0.0) + jnp.log1p(jnp.exp(-jnp.abs(v)))
        dta = dt * neg_a
        run = jnp.dot(tri_ref[0], dta, precision=HIGHEST, preferred_element_type=F32)
        rev = jnp.dot(tri_ref[1], dta, precision=HIGHEST, preferred_element_type=F32)
        cum = jnp.where(is_fwd, run, rev)
        total = jnp.where(is_fwd, run[q - 1:q, :], rev[0:1, :])
        cum2 = cum * LOG2_E
        cum_ref[rs, :] = cum2
        tr_ref[c, 0] = (cum2 - jnp.log2(dt)).T
        tr_ref[c, 1] = (dt * jnp.exp(total - cum)).T


def _ssd_prep_call(l, dtr, dtb, alog, tri):
    layer = lambda width: pl.BlockSpec((1, 1, width), lambda t: (l, 0, 0))
    rows = PREP_CHUNKS * CHUNK
    return pl.pallas_call(
        _ssd_prep_kernel,
        grid=(N_CHUNKS // PREP_CHUNKS,),
        in_specs=[pl.BlockSpec((rows, DT_PAD), lambda t: (t, 0)), layer(DT_PAD), layer(DT_PAD),
                  pl.BlockSpec((2, CHUNK, CHUNK), lambda t: (0, 0, 0))],
        out_specs=[pl.BlockSpec((rows, DT_PAD), lambda t: (t, 0)),
                   pl.BlockSpec((PREP_CHUNKS, 2, DT_PAD, CHUNK), lambda t: (t, 0, 0, 0))],
        out_shape=[jax.ShapeDtypeStruct((T_ALL, DT_PAD), F32),
                   jax.ShapeDtypeStruct((N_CHUNKS, 2, DT_PAD, CHUNK), F32)],
        compiler_params=pltpu.CompilerParams(
            dimension_semantics=("arbitrary",), vmem_limit_bytes=VMEM_LIMIT),
        name="ssd_prep",
    )(dtr, dtb, alog, tri)


def _ssd_kernel(direction, *refs):
    forward = direction == 0
    if forward:
        x_ref, bc_ref, cum_ref, tr_ref, wsrc_ref, o_ref, wdst_ref, h_scr = refs
    else:
        (x_ref, bc_ref, cum_ref, tr_ref, yf_ref, z_ref, dsk_ref, gn_ref, wsrc_ref,
         o_ref, wdst_ref, h_scr) = refs
    wdst_ref[...] = wsrc_ref[0].astype(BF16)
    q = CHUNK
    edge = q - 1 if forward else 0

    @pl.when(pl.program_id(0) == 0)
    def _():
        h_scr[...] = jnp.zeros_like(h_scr)

    row = lax.broadcasted_iota(jnp.int32, (q, q), 0)
    col = lax.broadcasted_iota(jnp.int32, (q, q), 1)
    keep = (col <= row) if forward else (col >= row)
    low_lanes = col < SSD_HEAD_DIM

    for c in (range(STEP_CHUNKS) if forward else reversed(range(STEP_CHUNKS))):
        rs = slice(c * q, (c + 1) * q)
        cum2 = cum_ref[rs, :]
        cdt_t = tr_ref[c, 0]
        w_t = tr_ref[c, 1]
        x = x_ref[rs, :]
        x16 = x.astype(BF16)

        ys = []
        for g in range(SSD_GROUPS):
            bm = bc_ref[rs, g * D_STATE:(g + 1) * D_STATE]
            cm = bc_ref[rs, (SSD_GROUPS + g) * D_STATE:(SSD_GROUPS + g + 1) * D_STATE]
            bt = bm.T
            c16 = cm.astype(BF16)
            cb = jnp.dot(c16, bt.astype(BF16), preferred_element_type=F32)
            h_prev = [h_scr[PAIRS_PER_GROUP * g + j] for j in range(PAIRS_PER_GROUP)]
            y_off = jnp.dot(c16, jnp.concatenate(h_prev, axis=1).astype(BF16),
                            preferred_element_type=F32)
            for j in range(PAIRS_PER_GROUP):
                top, bot, ecol = [], [], []
                for r in range(2):
                    hh = SSD_HEADS * direction + HEADS_PER_GROUP * g + 2 * j + r
                    colb = jnp.broadcast_to(cum2[:, hh:hh + 1], (q, q))
                    seg = colb - cdt_t[hh:hh + 1, :]
                    decay_dt = jnp.exp2(jnp.where(keep, seg, -jnp.inf))
                    top.append((decay_dt * cb).astype(BF16))
                    bot.append((bt * w_t[hh:hh + 1, :]).astype(BF16))
                    ecol.append(jnp.exp2(colb))
                lhs = jnp.concatenate([jnp.concatenate(top, axis=1),
                                       jnp.concatenate(bot, axis=1)], axis=0)
                c0 = g * GROUP_DIM + j * LANES
                xp = x16[:, c0:c0 + LANES]
                zero = jnp.zeros_like(xp)
                rhs = jnp.concatenate([jnp.where(low_lanes, xp, zero),
                                       jnp.where(low_lanes, zero, xp)], axis=0)
                res = jnp.dot(lhs, rhs, preferred_element_type=F32)
                escale = jnp.where(low_lanes, ecol[0], ecol[1])
                ys.append(res[:q] + y_off[:, j * LANES:(j + 1) * LANES] * escale)
                h_scr[PAIRS_PER_GROUP * g + j] = h_prev[j] * escale[edge:edge + 1, :] + res[q:]
        y = jnp.concatenate(ys, axis=1)

        if forward:
            o_ref[rs, :] = y
        else:
            y = yf_ref[rs, :] + y + dsk_ref[0] * x
            u = y * _silu(z_ref[rs, :])
            outs = []
            for g in range(SSD_GROUPS):
                ug = u[:, g * GROUP_DIM:(g + 1) * GROUP_DIM]
                outs.append(ug * lax.rsqrt(jnp.mean(ug * ug, axis=-1, keepdims=True) + EPS))
            o_ref[rs, :] = (jnp.concatenate(outs, axis=1) * gn_ref[0]).astype(o_ref.dtype)


def _fwd_block(t):
    return jnp.where(t == 0, N_SSD_STEPS - 1, t - 1)


def _bwd_block(t):
    return N_SSD_STEPS - 1 - t


def _ssd_tri():
    lower = np.tril(np.ones((CHUNK, CHUNK), np.float32))
    return jnp.asarray(np.stack([lower, lower.T]))


def _ssd_call(l, direction, p, cum, tr, w_mlp, extra=None):
    block_of = _fwd_block if direction == 0 else _bwd_block
    w_rows, w_cols = w_mlp.shape[1:]
    cast_steps = N_SSD_STEPS - 1
    slab = w_rows // cast_steps
    slab_of = lambda t: jnp.minimum(t, cast_steps - 1)
    rows = lambda width, cblk: pl.BlockSpec((STEP_CHUNKS * CHUNK, width), lambda t: (block_of(t), cblk))
    layer = lambda width: pl.BlockSpec((1, 1, width), lambda t: (l, 0, 0))
    bc_width = 2 * SSD_GROUPS * D_STATE
    in_specs = [rows(D_SSD, X_COL // D_SSD), rows(bc_width, BC_COL // bc_width), rows(DT_PAD, 0),
                pl.BlockSpec((STEP_CHUNKS, 2, DT_PAD, CHUNK), lambda t: (block_of(t), 0, 0, 0))]
    args = [p, p, cum, tr]
    if direction == 0:
        out_dtype = F32
    else:
        yf, dsk, gn = extra
        in_specs += [rows(D_SSD, 0), rows(D_SSD, Z_COL // D_SSD), layer(D_SSD), layer(D_SSD)]
        args += [yf, p, dsk, gn]
        out_dtype = BF16
    in_specs.append(pl.BlockSpec((1, slab, w_cols), lambda t: (l, slab_of(t), 0)))
    args.append(w_mlp)
    return pl.pallas_call(
        functools.partial(_ssd_kernel, direction),
        grid=(N_SSD_STEPS,),
        in_specs=in_specs,
        out_specs=[rows(D_SSD, 0), pl.BlockSpec((slab, w_cols), lambda t: (slab_of(t), 0))],
        out_shape=[jax.ShapeDtypeStruct((T_ALL, D_SSD), out_dtype),
                   jax.ShapeDtypeStruct((w_rows, w_cols), BF16)],
        scratch_shapes=[pltpu.VMEM((SSD_GROUPS * PAIRS_PER_GROUP, D_STATE, LANES), F32)],
        compiler_params=pltpu.CompilerParams(
            dimension_semantics=("arbitrary",), vmem_limit_bytes=VMEM_LIMIT),
        name="ssd_fwd" if direction == 0 else "ssd_bwd",
    )(*args)


def _outproj_kernel(x_ref, f_ref, y_ref, w_ref, gt_ref, o_ref, w16_scr):
    i = pl.program_id(0)

    @pl.when(i == 0)
    def _():
        rows = 256

        def body(c, carry):
            rs = pl.ds(pl.multiple_of(c * rows, rows), rows)
            w16_scr[rs, :] = w_ref[0, rs, :].astype(BF16)
            return carry

        lax.fori_loop(0, D_MODEL // rows, body, 0)

    acc = (jnp.dot(f_ref[...], w16_scr[:D_FOURIER, :], preferred_element_type=F32)
           + jnp.dot(y_ref[...], w16_scr[D_FOURIER:, :], preferred_element_type=F32))
    gate = _pick(_is_ctx_rows(i, OUT_ROWS), gt_ref)
    o_ref[...] = x_ref[...] + gate * acc


def _outproj_call(l, xs, f, y, w_out, mod):
    rows = lambda width: pl.BlockSpec((OUT_ROWS, width), lambda i: (i, 0))
    return pl.pallas_call(
        _outproj_kernel,
        grid=(T_ALL // OUT_ROWS,),
        in_specs=[
            rows(D_MODEL), rows(D_FOURIER), rows(D_SSD),
            pl.BlockSpec((1, D_MODEL, D_MODEL), lambda i: (l, 0, 0), pipeline_mode=pl.Buffered(1)),
            pl.BlockSpec((8, D_MODEL), lambda i: (0, 2)),
        ],
        out_specs=rows(D_MODEL),
        out_shape=jax.ShapeDtypeStruct((T_ALL, D_MODEL), F32),
        scratch_shapes=[pltpu.VMEM((D_MODEL, D_MODEL), BF16)],
        compiler_params=pltpu.CompilerParams(
            dimension_semantics=("arbitrary",), vmem_limit_bytes=VMEM_LIMIT),
        name="out_proj",
    )(xs, f, y, w_out, mod)


def _mlp_kernel(final, x_ref, g_ref, sh_ref, sc_ref, gt_ref, w1_ref, w2_ref, gf_ref, o_ref, xn_scr):
    i = pl.program_id(0)
    j = pl.program_id(1)

    @pl.when(j == 0)
    def _():
        _norm_modulate_rows(x_ref, g_ref, sh_ref, sc_ref, xn_scr, i, MLP_ROWS)
        o_ref[...] = jnp.zeros_like(o_ref)

    h = jnp.dot(xn_scr[...], w1_ref[...], preferred_element_type=F32)
    h = jnp.square(jnp.maximum(h, 0.0)).astype(BF16)
    o_ref[...] += jnp.dot(h, w2_ref[...], preferred_element_type=F32)

    @pl.when(j == pl.num_programs(1) - 1)
    def _():
        gate = _pick(_is_ctx_rows(i, MLP_ROWS), gt_ref)
        y = x_ref[...] + gate * o_ref[...]
        if final:
            y = y * lax.rsqrt(jnp.mean(y * y, axis=-1, keepdims=True) + EPS) * gf_ref[...]
        o_ref[...] = y


def _mlp_call(l, xs, g_mlp, mod, w1, w2, g_final, final):
    return pl.pallas_call(
        functools.partial(_mlp_kernel, final),
        grid=(T_ALL // MLP_ROWS, D_FF // COL_TILE),
        in_specs=[
            pl.BlockSpec((MLP_ROWS, D_MODEL), lambda i, j: (i, 0)),
            pl.BlockSpec((1, 1, D_MODEL), lambda i, j: (l, 0, 0)),
            pl.BlockSpec((8, D_MODEL), lambda i, j: (0, 3)),
            pl.BlockSpec((8, D_MODEL), lambda i, j: (0, 4)),
            pl.BlockSpec((8, D_MODEL), lambda i, j: (0, 5)),
            pl.BlockSpec((D_MODEL, COL_TILE), lambda i, j: (0, j)),
            pl.BlockSpec((COL_TILE, D_MODEL), lambda i, j: (j, 0)),
            pl.BlockSpec((1, D_MODEL), lambda i, j: (0, 0)),
        ],
        out_specs=pl.BlockSpec((MLP_ROWS, D_MODEL), lambda i, j: (i, 0)),
        out_shape=jax.ShapeDtypeStruct((SEQ if final else T_ALL, D_MODEL), F32),
        scratch_shapes=[pltpu.VMEM((MLP_ROWS, D_MODEL), BF16)],
        compiler_params=pltpu.CompilerParams(
            dimension_semantics=("arbitrary", "arbitrary"), vmem_limit_bytes=VMEM_LIMIT),
        name="mlp",
    )(xs, g_mlp, mod, mod, mod, w1, w2, g_final)


def _mixer_layer(l, xs, mod, tabs, tri, g_mix, w_in, conv_w, conv_b, dtb, alog, dsk, g_ssd_norm,
                 w_fourier, w_out, w_mlp1, w_mlp2):
    p, dtr = _inproj_call(l, xs, g_mix, mod, w_in, conv_w, conv_b)
    f = _fourier_lat_call(l, p, w_fourier, tabs)
    f = _fourier_ctx_call(l, p, w_fourier, tabs, f)
    cum, tr = _ssd_prep_call(l, dtr, dtb, alog, tri)
    yf, w1 = _ssd_call(l, 0, p, cum, tr, w_mlp1)
    y, w2 = _ssd_call(l, 1, p, cum, tr, w_mlp2, extra=(yf, dsk, g_ssd_norm))
    return _outproj_call(l, xs, f, y, w_out, mod), w1, w2


def kernel(x, c, ctx, c_ctx, w_ada, b_ada, g_mix, w_in, conv_w, conv_b, dt_bias, a_log, d_skip,
           g_ssd_norm, w_fourier, w_out, g_mlp, w_mlp1, w_mlp2, g_final):
    assert x.shape == (1, SEQ, D_MODEL) and ctx.shape == (1, CTX_LEN, D_MODEL)
    tabs = _dft_tables()
    tri = _ssd_tri()

    xs = jnp.concatenate([x[0], ctx[0]], axis=0)
    cc = jnp.concatenate([c, c_ctx[None, :], jnp.zeros((ADA_ROWS - 2, D_MODEL), F32)], axis=0)
    mods = _ada_call(cc, w_ada, b_ada)

    pad_heads = ((0, 0), (0, 0), (0, DT_PAD - 2 * SSD_HEADS))
    dtb = jnp.pad(dt_bias.reshape(DEPTH, 1, 2 * SSD_HEADS), pad_heads)
    alog = jnp.pad(a_log.reshape(DEPTH, 1, 2 * SSD_HEADS), pad_heads)
    dsk = jnp.repeat(d_skip, SSD_HEAD_DIM, axis=1).reshape(DEPTH, 1, D_SSD)
    row3 = lambda a: a.reshape(DEPTH, 1, a.shape[-1])
    w_in16 = w_in.astype(BF16)

    for l in range(DEPTH):
        xs, w1, w2 = _mixer_layer(l, xs, mods[l], tabs, tri, row3(g_mix), w_in16, conv_w, row3(conv_b),
                                  dtb, alog, dsk, row3(g_ssd_norm), w_fourier, w_out, w_mlp1, w_mlp2)
        xs = _mlp_call(l, xs, row3(g_mlp), mods[l], w1, w2, g_final[None, :], final=l == DEPTH - 1)

    return xs[None]
```

```python
import functools

import numpy as np
import jax
import jax.numpy as jnp
from jax import lax
from jax.experimental import pallas as pl
from jax.experimental.pallas import tpu as pltpu

F32 = jnp.float32
BF16 = jnp.bfloat16
HIGHEST = lax.Precision.HIGHEST

D_MODEL = 2048
SEQ = 8192
CTX_LEN = 256
T_ALL = SEQ + CTX_LEN
DEPTH = 4
GRID_W = 64
D_FOURIER = 512
N_FGROUPS = 4
FG_DIM = 128
D_SSD = 1536
SSD_HEAD_DIM = 64
SSD_HEADS = 24
SSD_GROUPS = 4
HEADS_PER_GROUP = 6
PAIRS_PER_GROUP = HEADS_PER_GROUP // 2
GROUP_DIM = D_SSD // SSD_GROUPS
D_STATE = 128
D_CONV = 5
CONV_DIM = 2560
CHUNK = 128
N_CHUNKS = T_ALL // CHUNK
N_LAT_CHUNKS = SEQ // CHUNK
STEP_CHUNKS = CTX_LEN // CHUNK
N_SSD_STEPS = N_CHUNKS // STEP_CHUNKS
D_MAIN = D_FOURIER + D_SSD + CONV_DIM
D_IN_PROJ = D_MAIN + 2 * SSD_HEADS
DT_PAD = 128
D_FF = 4 * D_MODEL
EPS = 1e-6
LOG2_E = float(np.log2(np.e))

SUBLANES = 8
LANES = 128
COL_TILE = 512
FF_TILE = 1024
MLP_ROWS = 768
PROJ_ROWS = 1408
OUT_ROWS = 528
ADA_ROWS = 16
ROW_CHUNK = 128
PREP_CHUNKS = 6
FFT_N1 = 64
FFT_N2 = 128
VMEM_LIMIT = 56 * 1024 * 1024

N_PROJ_TILES = D_MAIN // COL_TILE
N_PROJ_ROW_BLOCKS = T_ALL // PROJ_ROWS
N_PROJ_STEPS = N_PROJ_ROW_BLOCKS * N_PROJ_TILES
FIRST_XBC_TILE = (D_FOURIER + D_SSD) // COL_TILE
Z_COL = 0
X_COL = D_SSD
BC_COL = 2 * D_SSD
F_COL = D_SSD + CONV_DIM


def _silu(v):
    return v * jax.nn.sigmoid(v)


def _is_ctx_rows(block_idx, rows_per_block):
    row = block_idx * rows_per_block + lax.broadcasted_iota(jnp.int32, (rows_per_block, 1), 0)
    return row >= SEQ


def _pick(is_ctx, ref):
    return jnp.where(is_ctx, ref[1:2, :], ref[0:1, :])


def _norm_modulate_rows(x_ref, g_ref, sh_ref, sc_ref, xn_scr, block_idx, rows_per_block):
    g = g_ref[0]

    def body(c, carry):
        r0 = pl.multiple_of(c * ROW_CHUNK, ROW_CHUNK)
        is_ctx = block_idx * rows_per_block + r0 >= SEQ
        x = x_ref[pl.ds(r0, ROW_CHUNK), :]
        y = x * lax.rsqrt(jnp.mean(x * x, axis=-1, keepdims=True) + EPS) * g
        y = y * (1.0 + _pick(is_ctx, sc_ref)) + _pick(is_ctx, sh_ref)
        xn_scr[pl.ds(r0, ROW_CHUNK), :] = y.astype(BF16)
        return carry

    lax.fori_loop(0, rows_per_block // ROW_CHUNK, body, 0)


def _ada_kernel(c_ref, w_ref, b_ref, o_ref):
    s = _silu(c_ref[...]).astype(BF16)
    o_ref[0] = jnp.dot(s, w_ref[0].astype(BF16), preferred_element_type=F32) + b_ref[0]


def _ada_call(cc, w_ada, b_ada):
    tn = 1024
    return pl.pallas_call(
        _ada_kernel,
        grid=(DEPTH, 6 * D_MODEL // tn),
        in_specs=[
            pl.BlockSpec((ADA_ROWS, D_MODEL), lambda l, j: (0, 0)),
            pl.BlockSpec((1, D_MODEL, tn), lambda l, j: (l, 0, j)),
            pl.BlockSpec((1, 1, tn), lambda l, j: (l, 0, j)),
        ],
        out_specs=pl.BlockSpec((1, ADA_ROWS, tn), lambda l, j: (l, 0, j)),
        out_shape=jax.ShapeDtypeStruct((DEPTH, ADA_ROWS, 6 * D_MODEL), F32),
        compiler_params=pltpu.CompilerParams(
            dimension_semantics=("arbitrary", "arbitrary"), vmem_limit_bytes=VMEM_LIMIT),
        name="ada_mod",
    )(cc, w_ada, b_ada.reshape(DEPTH, 1, 6 * D_MODEL))


def _conv_silu(v, w_ref, b_ref, group_rows, col_slice):
    rows, cols = v.shape
    tiles = group_rows // SUBLANES
    v4 = v.reshape(rows // group_rows, tiles, SUBLANES, cols)
    sub = lax.broadcasted_iota(jnp.int32, (1, 1, SUBLANES, cols), 2)
    zero_tile = jnp.zeros((rows // group_rows, 1, SUBLANES, cols), F32)
    half = D_CONV // 2
    w_ref = w_ref.at[:, :, col_slice]
    b_ref = b_ref.at[:, :, col_slice]
    acc = v4 * w_ref[0, half:half + 1, :]
    for k in range(D_CONV):
        off = k - half
        if off == 0:
            continue
        r = pltpu.roll(v4, (-off) % SUBLANES, axis=2)
        if off < 0:
            nbr = jnp.concatenate([zero_tile, r[:, :-1]], axis=1)
            shifted = jnp.where(sub < -off, nbr, r)
        else:
            nbr = jnp.concatenate([r[:, 1:], zero_tile], axis=1)
            shifted = jnp.where(sub < SUBLANES - off, r, nbr)
        acc = acc + shifted * w_ref[0, k:k + 1, :]
    return _silu(acc + b_ref[0]).reshape(rows, cols)


def _inproj_kernel(x_ref, g_ref, sh_ref, sc_ref, w_ref, wdt_ref, cw_ref, cb_ref, p_ref, dt_ref, xn_scr):
    i = pl.program_id(0)
    j = pl.program_id(1)
    last = pl.num_programs(0) - 1

    @pl.when(j == 0)
    def _():
        _norm_modulate_rows(x_ref, g_ref, sh_ref, sc_ref, xn_scr, i, PROJ_ROWS)
        lane = lax.broadcasted_iota(jnp.int32, (1, DT_PAD), 1)
        wdt = jnp.where(lane < 2 * SSD_HEADS, wdt_ref[0], jnp.zeros((), BF16))
        dt_ref[...] = jnp.dot(xn_scr[...], wdt, preferred_element_type=F32)

    def matmul():
        return jnp.dot(xn_scr[...], w_ref[0], preferred_element_type=F32)

    all_cols = slice(0, COL_TILE)

    @pl.when(j < FIRST_XBC_TILE)
    def _():
        p_ref[...] = matmul()

    @pl.when(jnp.logical_and(j >= FIRST_XBC_TILE, i < last))
    def _():
        bounds = (0, 384, 768, 1088, PROJ_ROWS)
        for r0, r1 in zip(bounds[:-1], bounds[1:]):
            acc = jnp.dot(xn_scr[r0:r1, :], w_ref[0], preferred_element_type=F32)
            p_ref[r0:r1, :] = _conv_silu(acc, cw_ref, cb_ref, GRID_W, all_cols)

    @pl.when(jnp.logical_and(j >= FIRST_XBC_TILE, i == last))
    def _():
        n_lat = PROJ_ROWS - CTX_LEN
        acc = matmul()
        p_ref[:n_lat, :] = _conv_silu(acc[:n_lat], cw_ref, cb_ref, GRID_W, all_cols)
        p_ref[n_lat:, :] = _conv_silu(acc[n_lat:], cw_ref, cb_ref, CTX_LEN, all_cols)


def _proj_col_block(j):
    return jnp.where(j == 0, N_PROJ_TILES - 1, j - 1)


def _inproj_call(l, xs, g_mix, mod, w_in, conv_w, conv_b):
    xbc_tile = lambda j: jnp.maximum(j - FIRST_XBC_TILE, 0)
    return pl.pallas_call(
        _inproj_kernel,
        grid=(N_PROJ_ROW_BLOCKS, N_PROJ_TILES),
        in_specs=[
            pl.BlockSpec((PROJ_ROWS, D_MODEL), lambda i, j: (i, 0)),
            pl.BlockSpec((1, 1, D_MODEL), lambda i, j: (l, 0, 0)),
            pl.BlockSpec((8, D_MODEL), lambda i, j: (0, 0)),
            pl.BlockSpec((8, D_MODEL), lambda i, j: (0, 1)),
            pl.BlockSpec((1, D_MODEL, COL_TILE), lambda i, j: (l, 0, j)),
            pl.BlockSpec((1, D_MODEL, DT_PAD), lambda i, j: (l, 0, D_MAIN // DT_PAD)),
            pl.BlockSpec((1, D_CONV, COL_TILE), lambda i, j: (l, 0, xbc_tile(j))),
            pl.BlockSpec((1, 1, COL_TILE), lambda i, j: (l, 0, xbc_tile(j))),
        ],
        out_specs=[
            pl.BlockSpec((PROJ_ROWS, COL_TILE), lambda i, j: (i, _proj_col_block(j))),
            pl.BlockSpec((PROJ_ROWS, DT_PAD), lambda i, j: (i, 0)),
        ],
        out_shape=[
            jax.ShapeDtypeStruct((T_ALL, D_MAIN), F32),
            jax.ShapeDtypeStruct((T_ALL, DT_PAD), F32),
        ],
        scratch_shapes=[pltpu.VMEM((PROJ_ROWS, D_MODEL), BF16)],
        compiler_params=pltpu.CompilerParams(
            dimension_semantics=("arbitrary", "arbitrary"), vmem_limit_bytes=VMEM_LIMIT),
        name="in_proj",
    )(xs, g_mix, mod, mod, w_in, w_in, conv_w, conv_b)


def _dft_tables():
    L, n1, n2 = SEQ, FFT_N1, FFT_N2
    a = np.arange(n1)[:, None, None]
    k2 = np.arange(n2)[None, :, None]
    b = np.arange(n2)[None, None, :]
    ang = 2.0 * np.pi * ((k2 * (a + n1 * b)) % L) / L
    t1 = np.concatenate([np.cos(ang), -np.sin(ang)], axis=1)
    k1 = np.arange(n1)[:, None]
    aa = np.arange(n1)[None, :]
    ang2 = 2.0 * np.pi * ((k1 * aa) % n1) / n1
    c2, s2 = np.cos(ang2), np.sin(ang2)
    f2 = np.block([[c2, s2], [-s2, c2]])
    cc = np.arange(FG_DIM)
    angc = 2.0 * np.pi * ((cc[:, None] * cc[None, :]) % FG_DIM) / FG_DIM
    lc = np.arange(CTX_LEN)
    angl = 2.0 * np.pi * ((lc[:, None] * lc[None, :]) % CTX_LEN) / CTX_LEN
    as32 = lambda v: jnp.asarray(v, dtype=F32)
    as16 = lambda v: jnp.asarray(v, dtype=F32).astype(BF16)
    return dict(t1=as16(t1), f2=as16(f2), cc=as32(np.cos(angc)), sc=as32(np.sin(angc)),
                cl=as32(np.cos(angl)), sl=as32(np.sin(angl)))


def _fourier_lat_kernel(u_ref, t1_ref, f2_ref, cc_ref, sc_ref, w_ref, o_ref,
                        zr_scr, zi_scr, xr_scr, xi_scr):
    n1, n2 = FFT_N1, FFT_N2
    scale = 1.0 / np.sqrt(float(SEQ) * FG_DIM)
    w = w_ref[0, 0]
    g1 = jnp.dot(cc_ref[...], w, precision=HIGHEST, preferred_element_type=F32) * scale
    g2 = jnp.dot(sc_ref[...], w, precision=HIGHEST, preferred_element_type=F32) * scale
    gmat = jnp.concatenate([g1, g2], axis=0).astype(BF16)

    batch = SUBLANES

    def stage1(a, carry):
        xa = u_ref[pl.ds(a, n2, stride=n1), :].astype(BF16)
        z = jnp.dot(t1_ref[a], xa, preferred_element_type=F32)
        zr_scr[a] = z[:n2]
        zi_scr[a] = z[n2:]
        return carry

    lax.fori_loop(0, n1, stage1, 0, unroll=8)

    def stage2(kb, carry):
        k2 = pl.multiple_of(kb * batch, batch)
        zr = jnp.swapaxes(zr_scr[:, pl.ds(k2, batch), :], 0, 1)
        zi = jnp.swapaxes(zi_scr[:, pl.ds(k2, batch), :], 0, 1)
        cols = [jnp.concatenate([zr[q], zi[q]], axis=0).astype(BF16) for q in range(batch)]
        rhs = jnp.concatenate(cols, axis=1)
        res = jnp.dot(f2_ref[...], rhs, preferred_element_type=F32)
        for q in range(batch):
            blk = res[:, q * FG_DIM:(q + 1) * FG_DIM]
            xr_scr[k2 + q] = blk[:n1]
            xi_scr[k2 + q] = blk[n1:]
        return carry

    lax.fori_loop(0, n2 // batch, stage2, 0, unroll=2)

    def finish(kb, carry):
        k1 = pl.multiple_of(kb * batch, batch)
        xr = jnp.swapaxes(xr_scr[:, pl.ds(k1, batch), :], 0, 1)
        xi = jnp.swapaxes(xi_scr[:, pl.ds(k1, batch), :], 0, 1)
        for q in range(batch):
            xri = jnp.concatenate([xr[q], xi[q]], axis=1).astype(BF16)
            o = jnp.dot(xri, gmat, preferred_element_type=F32)
            row0 = pl.multiple_of((k1 + q) * n2, n2)
            o_ref[pl.ds(row0, n2), :] = o.astype(o_ref.dtype)
        return carry

    lax.fori_loop(0, n1 // batch, finish, 0)


def _fourier_lat_call(l, p, w_f, tabs):
    col0 = F_COL // FG_DIM
    const = lambda shape: pl.BlockSpec(shape, lambda g: (0,) * len(shape))
    return pl.pallas_call(
        _fourier_lat_kernel,
        grid=(N_FGROUPS,),
        in_specs=[
            pl.BlockSpec((SEQ, FG_DIM), lambda g: (0, col0 + g)),
            const((FFT_N1, 2 * FFT_N2, FFT_N2)),
            const((2 * FFT_N1, 2 * FFT_N1)),
            const((FG_DIM, FG_DIM)),
            const((FG_DIM, FG_DIM)),
            pl.BlockSpec((1, 1, FG_DIM, FG_DIM), lambda g: (l, g, 0, 0)),
        ],
        out_specs=pl.BlockSpec((SEQ, FG_DIM), lambda g: (0, g)),
        out_shape=jax.ShapeDtypeStruct((T_ALL, D_FOURIER), BF16),
        scratch_shapes=([pltpu.VMEM((FFT_N1, FFT_N2, FG_DIM), F32)] * 2
                        + [pltpu.VMEM((FFT_N2, FFT_N1, FG_DIM), F32)] * 2),
        compiler_params=pltpu.CompilerParams(
            dimension_semantics=("arbitrary",), vmem_limit_bytes=VMEM_LIMIT),
        name="fourier_lat",
    )(p, tabs["t1"], tabs["f2"], tabs["cc"], tabs["sc"], w_f)


def _fourier_ctx_kernel(u_ref, cl_ref, sl_ref, cc_ref, sc_ref, w_ref, f_hbm_ref, o_ref):
    del f_hbm_ref
    scale = 1.0 / np.sqrt(float(CTX_LEN) * FG_DIM)
    dot = functools.partial(jnp.dot, precision=HIGHEST, preferred_element_type=F32)
    w = w_ref[0, 0]
    u = u_ref[...]
    a = dot(u, dot(cc_ref[...], w))
    b = dot(u, dot(sc_ref[...], w))
    o_ref[...] = ((dot(cl_ref[...], a) - dot(sl_ref[...], b)) * scale).astype(o_ref.dtype)


def _fourier_ctx_call(l, p, w_f, tabs, f):
    col0 = F_COL // FG_DIM
    row_blk = SEQ // CTX_LEN
    const = lambda shape: pl.BlockSpec(shape, lambda g: (0,) * len(shape))
    return pl.pallas_call(
        _fourier_ctx_kernel,
        grid=(N_FGROUPS,),
        in_specs=[
            pl.BlockSpec((CTX_LEN, FG_DIM), lambda g: (row_blk, col0 + g)),
            const((CTX_LEN, CTX_LEN)),
            const((CTX_LEN, CTX_LEN)),
            const((FG_DIM, FG_DIM)),
            const((FG_DIM, FG_DIM)),
            pl.BlockSpec((1, 1, FG_DIM, FG_DIM), lambda g: (l, g, 0, 0)),
            pl.BlockSpec(memory_space=pl.ANY),
        ],
        out_specs=pl.BlockSpec((CTX_LEN, FG_DIM), lambda g: (row_blk, g)),
        out_shape=jax.ShapeDtypeStruct((T_ALL, D_FOURIER), BF16),
        input_output_aliases={6: 0},
        compiler_params=pltpu.CompilerParams(
            dimension_semantics=("arbitrary",), vmem_limit_bytes=VMEM_LIMIT),
        name="fourier_ctx",
    )(p, tabs["cl"], tabs["sl"], tabs["cc"], tabs["sc"], w_f, f)


def _ssd_prep_kernel(dtr_ref, dtb_ref, alog_ref, tri_ref, cum_ref, tr_ref):
    q = CHUNK
    is_fwd = lax.broadcasted_iota(jnp.int32, (1, DT_PAD), 1) < SSD_HEADS
    neg_a = -jnp.exp(alog_ref[0])
    for c in range(PREP_CHUNKS):
        rs = slice(c * q, (c + 1) * q)
        v = dtr_ref[rs, :] + dtb_ref[0]
        dt = jnp.maximum(v, 0.0) + jnp.log1p(jnp.exp(-jnp.abs(v)))
        dta = dt * neg_a
        run = jnp.dot(tri_ref[0], dta, precision=HIGHEST, preferred_element_type=F32)
        rev = jnp.dot(tri_ref[1], dta, precision=HIGHEST, preferred_element_type=F32)
        cum = jnp.where(is_fwd, run, rev)
        total = jnp.where(is_fwd, run[q - 1:q, :], rev[0:1, :])
        cum2 = cum * LOG2_E
        cum_ref[rs, :] = cum2
        tr_ref[c, 0] = (cum2 - jnp.log2(dt)).T
        tr_ref[c, 1] = (dt * jnp.exp(total - cum)).T


def _ssd_prep_call(l, dtr, dtb, alog, tri):
    layer = lambda width: pl.BlockSpec((1, 1, width), lambda t: (l, 0, 0))
    rows = PREP_CHUNKS * CHUNK
    return pl.pallas_call(
        _ssd_prep_kernel,
        grid=(N_CHUNKS // PREP_CHUNKS,),
        in_specs=[pl.BlockSpec((rows, DT_PAD), lambda t: (t, 0)), layer(DT_PAD), layer(DT_PAD),
                  pl.BlockSpec((2, CHUNK, CHUNK), lambda t: (0, 0, 0))],
        out_specs=[pl.BlockSpec((rows, DT_PAD), lambda t: (t, 0)),
                   pl.BlockSpec((PREP_CHUNKS, 2, DT_PAD, CHUNK), lambda t: (t, 0, 0, 0))],
        out_shape=[jax.ShapeDtypeStruct((T_ALL, DT_PAD), F32),
                   jax.ShapeDtypeStruct((N_CHUNKS, 2, DT_PAD, CHUNK), F32)],
        compiler_params=pltpu.CompilerParams(
            dimension_semantics=("arbitrary",), vmem_limit_bytes=VMEM_LIMIT),
        name="ssd_prep",
    )(dtr, dtb, alog, tri)


def _ssd_kernel(direction, *refs):
    forward = direction == 0
    if forward:
        x_ref, bc_ref, cum_ref, tr_ref, wsrc_ref, o_ref, wdst_ref, h_scr = refs
    else:
        (x_ref, bc_ref, cum_ref, tr_ref, yf_ref, z_ref, dsk_ref, gn_ref, wsrc_ref,
         o_ref, wdst_ref, h_scr) = refs
    wdst_ref[...] = wsrc_ref[0].astype(BF16)
    q = CHUNK
    edge = q - 1 if forward else 0

    @pl.when(pl.program_id(0) == 0)
    def _():
        h_scr[...] = jnp.zeros_like(h_scr)

    row = lax.broadcasted_iota(jnp.int32, (q, q), 0)
    col = lax.broadcasted_iota(jnp.int32, (q, q), 1)
    keep = (col <= row) if forward else (col >= row)
    low_lanes = col < SSD_HEAD_DIM

    for c in (range(STEP_CHUNKS) if forward else reversed(range(STEP_CHUNKS))):
        rs = slice(c * q, (c + 1) * q)
        cum2 = cum_ref[rs, :]
        cdt_t = tr_ref[c, 0]
        w_t = tr_ref[c, 1]
        x = x_ref[rs, :]
        x16 = x.astype(BF16)

        ys = []
        for g in range(SSD_GROUPS):
            bm = bc_ref[rs, g * D_STATE:(g + 1) * D_STATE]
            cm = bc_ref[rs, (SSD_GROUPS + g) * D_STATE:(SSD_GROUPS + g + 1) * D_STATE]
            bt = bm.T
            c16 = cm.astype(BF16)
            cb = jnp.dot(c16, bt.astype(BF16), preferred_element_type=F32)
            h_prev = [h_scr[PAIRS_PER_GROUP * g + j] for j in range(PAIRS_PER_GROUP)]
            y_off = jnp.dot(c16, jnp.concatenate(h_prev, axis=1).astype(BF16),
                            preferred_element_type=F32)
            for j in range(PAIRS_PER_GROUP):
                top, bot, ecol = [], [], []
                for r in range(2):
                    hh = SSD_HEADS * direction + HEADS_PER_GROUP * g + 2 * j + r
                    colb = jnp.broadcast_to(cum2[:, hh:hh + 1], (q, q))
                    seg = colb - cdt_t[hh:hh + 1, :]
                    decay_dt = jnp.exp2(jnp.where(keep, seg, -jnp.inf))
                    top.append((decay_dt * cb).astype(BF16))
                    bot.append((bt * w_t[hh:hh + 1, :]).astype(BF16))
                    ecol.append(jnp.exp2(colb))
                lhs = jnp.concatenate([jnp.concatenate(top, axis=1),
                                       jnp.concatenate(bot, axis=1)], axis=0)
                c0 = g * GROUP_DIM + j * LANES
                xp = x16[:, c0:c0 + LANES]
                zero = jnp.zeros_like(xp)
                rhs = jnp.concatenate([jnp.where(low_lanes, xp, zero),
                                       jnp.where(low_lanes, zero, xp)], axis=0)
                res = jnp.dot(lhs, rhs, preferred_element_type=F32)
                escale = jnp.where(low_lanes, ecol[0], ecol[1])
                ys.append(res[:q] + y_off[:, j * LANES:(j + 1) * LANES] * escale)
                h_scr[PAIRS_PER_GROUP * g + j] = h_prev[j] * escale[edge:edge + 1, :] + res[q:]
        y = jnp.concatenate(ys, axis=1)

        if forward:
            o_ref[rs, :] = y
        else:
            y = yf_ref[rs, :] + y + dsk_ref[0] * x
            u = y * _silu(z_ref[rs, :])
            outs = []
            for g in range(SSD_GROUPS):
                ug = u[:, g * GROUP_DIM:(g + 1) * GROUP_DIM]
                outs.append(ug * lax.rsqrt(jnp.mean(ug * ug, axis=-1, keepdims=True) + EPS))
            o_ref[rs, :] = (jnp.concatenate(outs, axis=1) * gn_ref[0]).astype(o_ref.dtype)


def _fwd_block(t):
    return jnp.where(t == 0, N_SSD_STEPS - 1, t - 1)


def _bwd_block(t):
    return N_SSD_STEPS - 1 - t


def _ssd_tri():
    lower = np.tril(np.ones((CHUNK, CHUNK), np.float32))
    return jnp.asarray(np.stack([lower, lower.T]))


def _ssd_call(l, direction, p, cum, tr, w_mlp, extra=None):
    block_of = _fwd_block if direction == 0 else _bwd_block
    w_rows, w_cols = w_mlp.shape[1:]
    cast_steps = N_SSD_STEPS - 1
    slab = w_rows // cast_steps
    slab_of = lambda t: jnp.minimum(t, cast_steps - 1)
    rows = lambda width, cblk: pl.BlockSpec((STEP_CHUNKS * CHUNK, width), lambda t: (block_of(t), cblk))
    layer = lambda width: pl.BlockSpec((1, 1, width), lambda t: (l, 0, 0))
    bc_width = 2 * SSD_GROUPS * D_STATE
    in_specs = [rows(D_SSD, X_COL // D_SSD), rows(bc_width, BC_COL // bc_width), rows(DT_PAD, 0),
                pl.BlockSpec((STEP_CHUNKS, 2, DT_PAD, CHUNK), lambda t: (block_of(t), 0, 0, 0))]
    args = [p, p, cum, tr]
    if direction == 0:
        out_dtype = F32
    else:
        yf, dsk, gn = extra
        in_specs += [rows(D_SSD, 0), rows(D_SSD, Z_COL // D_SSD), layer(D_SSD), layer(D_SSD)]
        args += [yf, p, dsk, gn]
        out_dtype = BF16
    in_specs.append(pl.BlockSpec((1, slab, w_cols), lambda t: (l, slab_of(t), 0)))
    args.append(w_mlp)
    return pl.pallas_call(
        functools.partial(_ssd_kernel, direction),
        grid=(N_SSD_STEPS,),
        in_specs=in_specs,
        out_specs=[rows(D_SSD, 0), pl.BlockSpec((slab, w_cols), lambda t: (slab_of(t), 0))],
        out_shape=[jax.ShapeDtypeStruct((T_ALL, D_SSD), out_dtype),
                   jax.ShapeDtypeStruct((w_rows, w_cols), BF16)],
        scratch_shapes=[pltpu.VMEM((SSD_GROUPS * PAIRS_PER_GROUP, D_STATE, LANES), F32)],
        compiler_params=pltpu.CompilerParams(
            dimension_semantics=("arbitrary",), vmem_limit_bytes=VMEM_LIMIT),
        name="ssd_fwd" if direction == 0 else "ssd_bwd",
    )(*args)


def _outproj_kernel(x_ref, f_ref, y_ref, w_ref, gt_ref, o_ref, w16_scr):
    i = pl.program_id(0)

    @pl.when(i == 0)
    def _():
        rows = 256

        def body(c, carry):
            rs = pl.ds(pl.multiple_of(c * rows, rows), rows)
            w16_scr[rs, :] = w_ref[0, rs, :].astype(BF16)
            return carry

        lax.fori_loop(0, D_MODEL // rows, body, 0)

    acc = (jnp.dot(f_ref[...], w16_scr[:D_FOURIER, :], preferred_element_type=F32)
           + jnp.dot(y_ref[...], w16_scr[D_FOURIER:, :], preferred_element_type=F32))
    gate = _pick(_is_ctx_rows(i, OUT_ROWS), gt_ref)
    o_ref[...] = x_ref[...] + gate * acc


def _outproj_call(l, xs, f, y, w_out, mod):
    rows = lambda width: pl.BlockSpec((OUT_ROWS, width), lambda i: (i, 0))
    return pl.pallas_call(
        _outproj_kernel,
        grid=(T_ALL // OUT_ROWS,),
        in_specs=[
            rows(D_MODEL), rows(D_FOURIER), rows(D_SSD),
            pl.BlockSpec((1, D_MODEL, D_MODEL), lambda i: (l, 0, 0), pipeline_mode=pl.Buffered(1)),
            pl.BlockSpec((8, D_MODEL), lambda i: (0, 2)),
        ],
        out_specs=rows(D_MODEL),
        out_shape=jax.ShapeDtypeStruct((T_ALL, D_MODEL), F32),
        scratch_shapes=[pltpu.VMEM((D_MODEL, D_MODEL), BF16)],
        compiler_params=pltpu.CompilerParams(
            dimension_semantics=("arbitrary",), vmem_limit_bytes=VMEM_LIMIT),
        name="out_proj",
    )(xs, f, y, w_out, mod)


def _mlp_kernel(final, x_ref, g_ref, sh_ref, sc_ref, gt_ref, w1_ref, w2_ref, gf_ref, o_ref, xn_scr):
    i = pl.program_id(0)
    j = pl.program_id(1)

    @pl.when(j == 0)
    def _():
        _norm_modulate_rows(x_ref, g_ref, sh_ref, sc_ref, xn_scr, i, MLP_ROWS)
        o_ref[...] = jnp.zeros_like(o_ref)

    h = jnp.dot(xn_scr[...], w1_ref[...], preferred_element_type=F32)
    h = jnp.square(jnp.maximum(h, 0.0)).astype(BF16)
    o_ref[...] += jnp.dot(h, w2_ref[...], preferred_element_type=F32)

    @pl.when(j == pl.num_programs(1) - 1)
    def _():
        gate = _pick(_is_ctx_rows(i, MLP_ROWS), gt_ref)
        y = x_ref[...] + gate * o_ref[...]
        if final:
            y = y * lax.rsqrt(jnp.mean(y * y, axis=-1, keepdims=True) + EPS) * gf_ref[...]
        o_ref[...] = y


def _mlp_call(l, xs, g_mlp, mod, w1, w2, g_final, final):
    return pl.pallas_call(
        functools.partial(_mlp_kernel, final),
        grid=(T_ALL // MLP_ROWS, D_FF // FF_TILE),
        in_specs=[
            pl.BlockSpec((MLP_ROWS, D_MODEL), lambda i, j: (i, 0)),
            pl.BlockSpec((1, 1, D_MODEL), lambda i, j: (l, 0, 0)),
            pl.BlockSpec((8, D_MODEL), lambda i, j: (0, 3)),
            pl.BlockSpec((8, D_MODEL), lambda i, j: (0, 4)),
            pl.BlockSpec((8, D_MODEL), lambda i, j: (0, 5)),
            pl.BlockSpec((D_MODEL, FF_TILE), lambda i, j: (0, j)),
            pl.BlockSpec((FF_TILE, D_MODEL), lambda i, j: (j, 0)),
            pl.BlockSpec((1, D_MODEL), lambda i, j: (0, 0)),
        ],
        out_specs=pl.BlockSpec((MLP_ROWS, D_MODEL), lambda i, j: (i, 0)),
        out_shape=jax.ShapeDtypeStruct((SEQ if final else T_ALL, D_MODEL), F32),
        scratch_shapes=[pltpu.VMEM((MLP_ROWS, D_MODEL), BF16)],
        compiler_params=pltpu.CompilerParams(
            dimension_semantics=("arbitrary", "arbitrary"), vmem_limit_bytes=VMEM_LIMIT),
        name="mlp",
    )(xs, g_mlp, mod, mod, mod, w1, w2, g_final)


def _mixer_layer(l, xs, mod, tabs, tri, g_mix, w_in, conv_w, conv_b, dtb, alog, dsk, g_ssd_norm,
                 w_fourier, w_out, w_mlp1, w_mlp2):
    p, dtr = _inproj_call(l, xs, g_mix, mod, w_in, conv_w, conv_b)
    f = _fourier_lat_call(l, p, w_fourier, tabs)
    f = _fourier_ctx_call(l, p, w_fourier, tabs, f)
    cum, tr = _ssd_prep_call(l, dtr, dtb, alog, tri)
    yf, w1 = _ssd_call(l, 0, p, cum, tr, w_mlp1)
    y, w2 = _ssd_call(l, 1, p, cum, tr, w_mlp2, extra=(yf, dsk, g_ssd_norm))
    return _outproj_call(l, xs, f, y, w_out, mod), w1, w2


def kernel(x, c, ctx, c_ctx, w_ada, b_ada, g_mix, w_in, conv_w, conv_b, dt_bias, a_log, d_skip,
           g_ssd_norm, w_fourier, w_out, g_mlp, w_mlp1, w_mlp2, g_final):
    assert x.shape == (1, SEQ, D_MODEL) and ctx.shape == (1, CTX_LEN, D_MODEL)
    tabs = _dft_tables()
    tri = _ssd_tri()

    xs = jnp.concatenate([x[0], ctx[0]], axis=0)
    cc = jnp.concatenate([c, c_ctx[None, :], jnp.zeros((ADA_ROWS - 2, D_MODEL), F32)], axis=0)
    mods = _ada_call(cc, w_ada, b_ada)

    pad_heads = ((0, 0), (0, 0), (0, DT_PAD - 2 * SSD_HEADS))
    dtb = jnp.pad(dt_bias.reshape(DEPTH, 1, 2 * SSD_HEADS), pad_heads)
    alog = jnp.pad(a_log.reshape(DEPTH, 1, 2 * SSD_HEADS), pad_heads)
    dsk = jnp.repeat(d_skip, SSD_HEAD_DIM, axis=1).reshape(DEPTH, 1, D_SSD)
    row3 = lambda a: a.reshape(DEPTH, 1, a.shape[-1])
    w_in16 = w_in.astype(BF16)

    for l in range(DEPTH):
        xs, w1, w2 = _mixer_layer(l, xs, mods[l], tabs, tri, row3(g_mix), w_in16, conv_w, row3(conv_b),
                                  dtb, alog, dsk, row3(g_ssd_norm), w_fourier, w_out, w_mlp1, w_mlp2)
        xs = _mlp_call(l, xs, row3(g_mlp), mods[l], w1, w2, g_final[None, :], final=l == DEPTH - 1)

    return xs[None]
```

```python
import functools
import math

import numpy as np
import jax
import jax.numpy as jnp
from jax import lax
from jax.experimental import pallas as pl
from jax.experimental.pallas import tpu as pltpu

F32 = jnp.float32
BF16 = jnp.bfloat16
HIGHEST = lax.Precision.HIGHEST

D_MODEL = 2048
SEQ = 8192
CTX_LEN = 256
T_ALL = SEQ + CTX_LEN
DEPTH = 4
GRID_W = 64
D_FOURIER = 512
N_FGROUPS = 4
FG_DIM = 128
D_SSD = 1536
SSD_HEAD_DIM = 64
SSD_HEADS = 24
SSD_GROUPS = 4
HEADS_PER_GROUP = 6
PAIRS_PER_GROUP = HEADS_PER_GROUP // 2
GROUP_DIM = D_SSD // SSD_GROUPS
D_STATE = 128
D_CONV = 5
CONV_DIM = 2560
CHUNK = 128
N_CHUNKS = T_ALL // CHUNK
N_LAT_CHUNKS = SEQ // CHUNK
STEP_CHUNKS = CTX_LEN // CHUNK
N_SSD_STEPS = N_CHUNKS // STEP_CHUNKS
D_MAIN = D_FOURIER + D_SSD + CONV_DIM
D_IN_PROJ = D_MAIN + 2 * SSD_HEADS
DT_PAD = 128
D_FF = 4 * D_MODEL
EPS = 1e-6
LOG2_E = float(np.log2(np.e))

SUBLANES = 8
LANES = 128
COL_TILE = 512
FF_TILE = 1024
MLP_ROWS = 768
PROJ_ROWS = 1408
OUT_ROWS = 528
ADA_ROWS = 16
ROW_CHUNK = 128
PREP_CHUNKS = 6
FFT_N1 = 64
FFT_N2 = 128
VMEM_LIMIT = 56 * 1024 * 1024

N_PROJ_TILES = D_MAIN // COL_TILE
N_PROJ_ROW_BLOCKS = T_ALL // PROJ_ROWS
N_PROJ_STEPS = N_PROJ_ROW_BLOCKS * N_PROJ_TILES
FIRST_XBC_TILE = (D_FOURIER + D_SSD) // COL_TILE
Z_COL = 0
X_COL = D_SSD
BC_COL = 2 * D_SSD
F_COL = D_SSD + CONV_DIM


def _silu(v):
    return v * jax.nn.sigmoid(v)


def _is_ctx_rows(block_idx, rows_per_block):
    row = block_idx * rows_per_block + lax.broadcasted_iota(jnp.int32, (rows_per_block, 1), 0)
    return row >= SEQ


def _pick(is_ctx, ref):
    return jnp.where(is_ctx, ref[1:2, :], ref[0:1, :])


def _norm_modulate_rows(x_ref, ctx_ref, g_ref, sh_ref, sc_ref, xn_scr, block_idx, rows_per_block):
    g = g_ref[0]
    chunk = math.gcd(rows_per_block, ROW_CHUNK)
    assert SEQ % chunk == 0

    def body(c, carry):
        r0 = pl.multiple_of(c * chunk, chunk)
        row = block_idx * rows_per_block + r0
        is_ctx = row >= SEQ
        if ctx_ref is None:
            x = x_ref[pl.ds(r0, chunk), :]
        else:
            x = lax.cond(is_ctx,
                         lambda: ctx_ref[pl.ds(pl.multiple_of(row - SEQ, chunk), chunk), :],
                         lambda: x_ref[pl.ds(r0, chunk), :])
        y = x * lax.rsqrt(jnp.mean(x * x, axis=-1, keepdims=True) + EPS) * g
        y = y * (1.0 + _pick(is_ctx, sc_ref)) + _pick(is_ctx, sh_ref)
        xn_scr[pl.ds(r0, chunk), :] = y.astype(BF16)
        return carry

    lax.fori_loop(0, rows_per_block // chunk, body, 0)


def _ada_kernel(c_ref, w_ref, b_ref, o_ref):
    s = _silu(c_ref[...]).astype(BF16)
    o_ref[0] = jnp.dot(s, w_ref[0].astype(BF16), preferred_element_type=F32) + b_ref[0]


def _ada_call(cc, w_ada, b_ada):
    tn = 1024
    return pl.pallas_call(
        _ada_kernel,
        grid=(DEPTH, 6 * D_MODEL // tn),
        in_specs=[
            pl.BlockSpec((ADA_ROWS, D_MODEL), lambda l, j: (0, 0)),
            pl.BlockSpec((1, D_MODEL, tn), lambda l, j: (l, 0, j)),
            pl.BlockSpec((1, 1, tn), lambda l, j: (l, 0, j)),
        ],
        out_specs=pl.BlockSpec((1, ADA_ROWS, tn), lambda l, j: (l, 0, j)),
        out_shape=jax.ShapeDtypeStruct((DEPTH, ADA_ROWS, 6 * D_MODEL), F32),
        compiler_params=pltpu.CompilerParams(
            dimension_semantics=("arbitrary", "arbitrary"), vmem_limit_bytes=VMEM_LIMIT),
        name="ada_mod",
    )(cc, w_ada, b_ada.reshape(DEPTH, 1, 6 * D_MODEL))


def _conv_silu(v, w_ref, b_ref, group_rows, col_slice):
    rows, cols = v.shape
    tiles = group_rows // SUBLANES
    v4 = v.reshape(rows // group_rows, tiles, SUBLANES, cols)
    sub = lax.broadcasted_iota(jnp.int32, (1, 1, SUBLANES, cols), 2)
    zero_tile = jnp.zeros((rows // group_rows, 1, SUBLANES, cols), F32)
    half = D_CONV // 2
    w_ref = w_ref.at[:, :, col_slice]
    b_ref = b_ref.at[:, :, col_slice]
    acc = v4 * w_ref[0, half:half + 1, :]
    for k in range(D_CONV):
        off = k - half
        if off == 0:
            continue
        r = pltpu.roll(v4, (-off) % SUBLANES, axis=2)
        if off < 0:
            nbr = jnp.concatenate([zero_tile, r[:, :-1]], axis=1)
            shifted = jnp.where(sub < -off, nbr, r)
        else:
            nbr = jnp.concatenate([r[:, 1:], zero_tile], axis=1)
            shifted = jnp.where(sub < SUBLANES - off, r, nbr)
        acc = acc + shifted * w_ref[0, k:k + 1, :]
    return _silu(acc + b_ref[0]).reshape(rows, cols)


def _inproj_kernel(split, *refs):
    if split:
        x_ref, ctx_ref, g_ref, sh_ref, sc_ref, w_ref, wdt_ref, cw_ref, cb_ref, p_ref, dt_ref, xn_scr = refs
    else:
        x_ref, g_ref, sh_ref, sc_ref, w_ref, wdt_ref, cw_ref, cb_ref, p_ref, dt_ref, xn_scr = refs
        ctx_ref = None
    i = pl.program_id(0)
    j = pl.program_id(1)
    last = pl.num_programs(0) - 1

    @pl.when(j == 0)
    def _():
        _norm_modulate_rows(x_ref, ctx_ref, g_ref, sh_ref, sc_ref, xn_scr, i, PROJ_ROWS)
        lane = lax.broadcasted_iota(jnp.int32, (1, DT_PAD), 1)
        wdt = jnp.where(lane < 2 * SSD_HEADS, wdt_ref[0], jnp.zeros((), BF16))
        dt_ref[...] = jnp.dot(xn_scr[...], wdt, preferred_element_type=F32)

    def matmul():
        return jnp.dot(xn_scr[...], w_ref[0], preferred_element_type=F32)

    all_cols = slice(0, COL_TILE)

    @pl.when(j < FIRST_XBC_TILE)
    def _():
        p_ref[...] = matmul()

    @pl.when(jnp.logical_and(j >= FIRST_XBC_TILE, i < last))
    def _():
        bounds = (0, 384, 768, 1088, PROJ_ROWS)
        for r0, r1 in zip(bounds[:-1], bounds[1:]):
            acc = jnp.dot(xn_scr[r0:r1, :], w_ref[0], preferred_element_type=F32)
            p_ref[r0:r1, :] = _conv_silu(acc, cw_ref, cb_ref, GRID_W, all_cols)

    @pl.when(jnp.logical_and(j >= FIRST_XBC_TILE, i == last))
    def _():
        n_lat = PROJ_ROWS - CTX_LEN
        acc = matmul()
        p_ref[:n_lat, :] = _conv_silu(acc[:n_lat], cw_ref, cb_ref, GRID_W, all_cols)
        p_ref[n_lat:, :] = _conv_silu(acc[n_lat:], cw_ref, cb_ref, CTX_LEN, all_cols)


def _proj_col_block(j):
    return jnp.where(j == 0, N_PROJ_TILES - 1, j - 1)


def _stream_specs(stream, rows, n_grid_axes):
    first = (lambda i: (i, 0)) if n_grid_axes == 1 else (lambda i, j: (i, 0))
    zero = (lambda i: (0, 0)) if n_grid_axes == 1 else (lambda i, j: (0, 0))
    if isinstance(stream, tuple):
        return [pl.BlockSpec((rows, D_MODEL), first), pl.BlockSpec((CTX_LEN, D_MODEL), zero)], list(stream)
    return [pl.BlockSpec((rows, D_MODEL), first)], [stream]


def _inproj_call(l, xs, g_mix, mod, w_in, conv_w, conv_b):
    xbc_tile = lambda j: jnp.maximum(j - FIRST_XBC_TILE, 0)
    stream_specs, stream_args = _stream_specs(xs, PROJ_ROWS, 2)
    return pl.pallas_call(
        functools.partial(_inproj_kernel, isinstance(xs, tuple)),
        grid=(N_PROJ_ROW_BLOCKS, N_PROJ_TILES),
        in_specs=stream_specs + [
            pl.BlockSpec((1, 1, D_MODEL), lambda i, j: (l, 0, 0)),
            pl.BlockSpec((8, D_MODEL), lambda i, j: (0, 0)),
            pl.BlockSpec((8, D_MODEL), lambda i, j: (0, 1)),
            pl.BlockSpec((1, D_MODEL, COL_TILE), lambda i, j: (l, 0, j)),
            pl.BlockSpec((1, D_MODEL, DT_PAD), lambda i, j: (l, 0, D_MAIN // DT_PAD)),
            pl.BlockSpec((1, D_CONV, COL_TILE), lambda i, j: (l, 0, xbc_tile(j))),
            pl.BlockSpec((1, 1, COL_TILE), lambda i, j: (l, 0, xbc_tile(j))),
        ],
        out_specs=[
            pl.BlockSpec((PROJ_ROWS, COL_TILE), lambda i, j: (i, _proj_col_block(j))),
            pl.BlockSpec((PROJ_ROWS, DT_PAD), lambda i, j: (i, 0)),
        ],
        out_shape=[
            jax.ShapeDtypeStruct((T_ALL, D_MAIN), F32),
            jax.ShapeDtypeStruct((T_ALL, DT_PAD), F32),
        ],
        scratch_shapes=[pltpu.VMEM((PROJ_ROWS, D_MODEL), BF16)],
        compiler_params=pltpu.CompilerParams(
            dimension_semantics=("arbitrary", "arbitrary"), vmem_limit_bytes=VMEM_LIMIT),
        name="in_proj",
    )(*stream_args, g_mix, mod, mod, w_in, w_in, conv_w, conv_b)


def _dft_tables():
    L, n1, n2 = SEQ, FFT_N1, FFT_N2
    a = np.arange(n1)[:, None, None]
    k2 = np.arange(n2)[None, :, None]
    b = np.arange(n2)[None, None, :]
    ang = 2.0 * np.pi * ((k2 * (a + n1 * b)) % L) / L
    t1 = np.concatenate([np.cos(ang), -np.sin(ang)], axis=1)
    k1 = np.arange(n1)[:, None]
    aa = np.arange(n1)[None, :]
    ang2 = 2.0 * np.pi * ((k1 * aa) % n1) / n1
    c2, s2 = np.cos(ang2), np.sin(ang2)
    f2 = np.block([[c2, s2], [-s2, c2]])
    cc = np.arange(FG_DIM)
    angc = 2.0 * np.pi * ((cc[:, None] * cc[None, :]) % FG_DIM) / FG_DIM
    lc = np.arange(CTX_LEN)
    angl = 2.0 * np.pi * ((lc[:, None] * lc[None, :]) % CTX_LEN) / CTX_LEN
    as32 = lambda v: jnp.asarray(v, dtype=F32)
    as16 = lambda v: jnp.asarray(v, dtype=F32).astype(BF16)
    return dict(t1=as16(t1), f2=as16(f2), cc=as32(np.cos(angc)), sc=as32(np.sin(angc)),
                cl=as32(np.cos(angl)), sl=as32(np.sin(angl)))


def _fourier_lat_kernel(u_ref, t1_ref, f2_ref, cc_ref, sc_ref, w_ref, o_ref,
                        zr_scr, zi_scr, xr_scr, xi_scr):
    n1, n2 = FFT_N1, FFT_N2
    scale = 1.0 / np.sqrt(float(SEQ) * FG_DIM)
    w = w_ref[0, 0]
    g1 = jnp.dot(cc_ref[...], w, precision=HIGHEST, preferred_element_type=F32) * scale
    g2 = jnp.dot(sc_ref[...], w, precision=HIGHEST, preferred_element_type=F32) * scale
    gmat = jnp.concatenate([g1, g2], axis=0).astype(BF16)

    batch = SUBLANES

    def stage1(a, carry):
        xa = u_ref[pl.ds(a, n2, stride=n1), :].astype(BF16)
        z = jnp.dot(t1_ref[a], xa, preferred_element_type=F32)
        zr_scr[a] = z[:n2]
        zi_scr[a] = z[n2:]
        return carry

    lax.fori_loop(0, n1, stage1, 0, unroll=8)

    def stage2(kb, carry):
        k2 = pl.multiple_of(kb * batch, batch)
        zr = jnp.swapaxes(zr_scr[:, pl.ds(k2, batch), :], 0, 1)
        zi = jnp.swapaxes(zi_scr[:, pl.ds(k2, batch), :], 0, 1)
        cols = [jnp.concatenate([zr[q], zi[q]], axis=0).astype(BF16) for q in range(batch)]
        rhs = jnp.concatenate(cols, axis=1)
        res = jnp.dot(f2_ref[...], rhs, preferred_element_type=F32)
        for q in range(batch):
            blk = res[:, q * FG_DIM:(q + 1) * FG_DIM]
            xr_scr[k2 + q] = blk[:n1]
            xi_scr[k2 + q] = blk[n1:]
        return carry

    lax.fori_loop(0, n2 // batch, stage2, 0, unroll=2)

    def finish(kb, carry):
        k1 = pl.multiple_of(kb * batch, batch)
        xr = jnp.swapaxes(xr_scr[:, pl.ds(k1, batch), :], 0, 1)
        xi = jnp.swapaxes(xi_scr[:, pl.ds(k1, batch), :], 0, 1)
        for q in range(batch):
            xri = jnp.concatenate([xr[q], xi[q]], axis=1).astype(BF16)
            o = jnp.dot(xri, gmat, preferred_element_type=F32)
            row0 = pl.multiple_of((k1 + q) * n2, n2)
            o_ref[pl.ds(row0, n2), :] = o.astype(o_ref.dtype)
        return carry

    lax.fori_loop(0, n1 // batch, finish, 0)


def _fourier_lat_call(l, p, w_f, tabs):
    col0 = F_COL // FG_DIM
    const = lambda shape: pl.BlockSpec(shape, lambda g: (0,) * len(shape))
    return pl.pallas_call(
        _fourier_lat_kernel,
        grid=(N_FGROUPS,),
        in_specs=[
            pl.BlockSpec((SEQ, FG_DIM), lambda g: (0, col0 + g)),
            const((FFT_N1, 2 * FFT_N2, FFT_N2)),
            const((2 * FFT_N1, 2 * FFT_N1)),
            const((FG_DIM, FG_DIM)),
            const((FG_DIM, FG_DIM)),
            pl.BlockSpec((1, 1, FG_DIM, FG_DIM), lambda g: (l, g, 0, 0)),
        ],
        out_specs=pl.BlockSpec((SEQ, FG_DIM), lambda g: (0, g)),
        out_shape=jax.ShapeDtypeStruct((T_ALL, D_FOURIER), BF16),
        scratch_shapes=([pltpu.VMEM((FFT_N1, FFT_N2, FG_DIM), F32)] * 2
                        + [pltpu.VMEM((FFT_N2, FFT_N1, FG_DIM), F32)] * 2),
        compiler_params=pltpu.CompilerParams(
            dimension_semantics=("arbitrary",), vmem_limit_bytes=VMEM_LIMIT),
        name="fourier_lat",
    )(p, tabs["t1"], tabs["f2"], tabs["cc"], tabs["sc"], w_f)


def _fourier_ctx_kernel(u_ref, cl_ref, sl_ref, cc_ref, sc_ref, w_ref, f_hbm_ref, o_ref):
    del f_hbm_ref
    scale = 1.0 / np.sqrt(float(CTX_LEN) * FG_DIM)
    dot = functools.partial(jnp.dot, precision=HIGHEST, preferred_element_type=F32)
    w = w_ref[0, 0]
    u = u_ref[...]
    a = dot(u, dot(cc_ref[...], w))
    b = dot(u, dot(sc_ref[...], w))
    o_ref[...] = ((dot(cl_ref[...], a) - dot(sl_ref[...], b)) * scale).astype(o_ref.dtype)


def _fourier_ctx_call(l, p, w_f, tabs, f):
    col0 = F_COL // FG_DIM
    row_blk = SEQ // CTX_LEN
    const = lambda shape: pl.BlockSpec(shape, lambda g: (0,) * len(shape))
    return pl.pallas_call(
        _fourier_ctx_kernel,
        grid=(N_FGROUPS,),
        in_specs=[
            pl.BlockSpec((CTX_LEN, FG_DIM), lambda g: (row_blk, col0 + g)),
            const((CTX_LEN, CTX_LEN)),
            const((CTX_LEN, CTX_LEN)),
            const((FG_DIM, FG_DIM)),
            const((FG_DIM, FG_DIM)),
            pl.BlockSpec((1, 1, FG_DIM, FG_DIM), lambda g: (l, g, 0, 0)),
            pl.BlockSpec(memory_space=pl.ANY),
        ],
        out_specs=pl.BlockSpec((CTX_LEN, FG_DIM), lambda g: (row_blk, g)),
        out_shape=jax.ShapeDtypeStruct((T_ALL, D_FOURIER), BF16),
        input_output_aliases={6: 0},
        compiler_params=pltpu.CompilerParams(
            dimension_semantics=("arbitrary",), vmem_limit_bytes=VMEM_LIMIT),
        name="fourier_ctx",
    )(p, tabs["cl"], tabs["sl"], tabs["cc"], tabs["sc"], w_f, f)


def _ssd_prep_kernel(dtr_ref, dtb_ref, alog_ref, tri_ref, cum_ref, tr_ref):
    q = CHUNK
    is_fwd = lax.broadcasted_iota(jnp.int32, (1, DT_PAD), 1) < SSD_HEADS
    neg_a = -jnp.exp(alog_ref[0])
    for c in range(PREP_CHUNKS):
        rs = slice(c * q, (c + 1) * q)
        v = dtr_ref[rs, :] + dtb_ref[0]
        dt = jnp.maximum(v, 0.0) + jnp.log1p(jnp.exp(-jnp.abs(v)))
        dta = dt * neg_a
        run = jnp.dot(tri_ref[0], dta, precision=HIGHEST, preferred_element_type=F32)
        rev = jnp.dot(tri_ref[1], dta, precision=HIGHEST, preferred_element_type=F32)
        cum = jnp.where(is_fwd, run, rev)
        total = jnp.where(is_fwd, run[q - 1:q, :], rev[0:1, :])
        cum2 = cum * LOG2_E
        cum_ref[rs, :] = cum2
        tr_ref[c, 0] = (cum2 - jnp.log2(dt)).T
        tr_ref[c, 1] = (dt * jnp.exp(total - cum)).T


def _ssd_prep_call(l, dtr, dtb, alog, tri):
    layer = lambda width: pl.BlockSpec((1, 1, width), lambda t: (l, 0, 0))
    rows = PREP_CHUNKS * CHUNK
    return pl.pallas_call(
        _ssd_prep_kernel,
        grid=(N_CHUNKS // PREP_CHUNKS,),
        in_specs=[pl.BlockSpec((rows, DT_PAD), lambda t: (t, 0)), layer(DT_PAD), layer(DT_PAD),
                  pl.BlockSpec((2, CHUNK, CHUNK), lambda t: (0, 0, 0))],
        out_specs=[pl.BlockSpec((rows, DT_PAD), lambda t: (t, 0)),
                   pl.BlockSpec((PREP_CHUNKS, 2, DT_PAD, CHUNK), lambda t: (t, 0, 0, 0))],
        out_shape=[jax.ShapeDtypeStruct((T_ALL, DT_PAD), F32),
                   jax.ShapeDtypeStruct((N_CHUNKS, 2, DT_PAD, CHUNK), F32)],
        compiler_params=pltpu.CompilerParams(
            dimension_semantics=("arbitrary",), vmem_limit_bytes=VMEM_LIMIT),
        name="ssd_prep",
    )(dtr, dtb, alog, tri)


def _ssd_kernel(direction, *refs):
    forward = direction == 0
    if forward:
        x_ref, bc_ref, cum_ref, tr_ref, wsrc_ref, o_ref, wdst_ref, h_scr = refs
    else:
        (x_ref, bc_ref, cum_ref, tr_ref, yf_ref, z_ref, dsk_ref, gn_ref, wsrc_ref,
         o_ref, wdst_ref, h_scr) = refs
    wdst_ref[...] = wsrc_ref[0].astype(BF16)
    q = CHUNK
    edge = q - 1 if forward else 0

    @pl.when(pl.program_id(0) == 0)
    def _():
        h_scr[...] = jnp.zeros_like(h_scr)

    row = lax.broadcasted_iota(jnp.int32, (q, q), 0)
    col = lax.broadcasted_iota(jnp.int32, (q, q), 1)
    keep = (col <= row) if forward else (col >= row)
    low_lanes = col < SSD_HEAD_DIM

    for c in (range(STEP_CHUNKS) if forward else reversed(range(STEP_CHUNKS))):
        rs = slice(c * q, (c + 1) * q)
        cum2 = cum_ref[rs, :]
        cdt_t = tr_ref[c, 0]
        w_t = tr_ref[c, 1]
        x = x_ref[rs, :]
        x16 = x.astype(BF16)

        ys = []
        for g in range(SSD_GROUPS):
            bm = bc_ref[rs, g * D_STATE:(g + 1) * D_STATE]
            cm = bc_ref[rs, (SSD_GROUPS + g) * D_STATE:(SSD_GROUPS + g + 1) * D_STATE]
            bt = bm.T
            c16 = cm.astype(BF16)
            cb = jnp.dot(c16, bt.astype(BF16), preferred_element_type=F32)
            h_prev = [h_scr[PAIRS_PER_GROUP * g + j] for j in range(PAIRS_PER_GROUP)]
            y_off = jnp.dot(c16, jnp.concatenate(h_prev, axis=1).astype(BF16),
                            preferred_element_type=F32)
            for j in range(PAIRS_PER_GROUP):
                top, bot, ecol = [], [], []
                for r in range(2):
                    hh = SSD_HEADS * direction + HEADS_PER_GROUP * g + 2 * j + r
                    colb = jnp.broadcast_to(cum2[:, hh:hh + 1], (q, q))
                    seg = colb - cdt_t[hh:hh + 1, :]
                    decay_dt = jnp.exp2(jnp.where(keep, seg, -jnp.inf))
                    top.append((decay_dt * cb).astype(BF16))
                    bot.append((bt * w_t[hh:hh + 1, :]).astype(BF16))
                    ecol.append(jnp.exp2(colb))
                lhs = jnp.concatenate([jnp.concatenate(top, axis=1),
                                       jnp.concatenate(bot, axis=1)], axis=0)
                c0 = g * GROUP_DIM + j * LANES
                xp = x16[:, c0:c0 + LANES]
                zero = jnp.zeros_like(xp)
                rhs = jnp.concatenate([jnp.where(low_lanes, xp, zero),
                                       jnp.where(low_lanes, zero, xp)], axis=0)
                res = jnp.dot(lhs, rhs, preferred_element_type=F32)
                escale = jnp.where(low_lanes, ecol[0], ecol[1])
                ys.append(res[:q] + y_off[:, j * LANES:(j + 1) * LANES] * escale)
                h_scr[PAIRS_PER_GROUP * g + j] = h_prev[j] * escale[edge:edge + 1, :] + res[q:]
        y = jnp.concatenate(ys, axis=1)

        if forward:
            o_ref[rs, :] = y
        else:
            y = yf_ref[rs, :] + y + dsk_ref[0] * x
            u = y * _silu(z_ref[rs, :])
            outs = []
            for g in range(SSD_GROUPS):
                ug = u[:, g * GROUP_DIM:(g + 1) * GROUP_DIM]
                outs.append(ug * lax.rsqrt(jnp.mean(ug * ug, axis=-1, keepdims=True) + EPS))
            o_ref[rs, :] = (jnp.concatenate(outs, axis=1) * gn_ref[0]).astype(o_ref.dtype)


def _fwd_block(t):
    return jnp.where(t == 0, N_SSD_STEPS - 1, t - 1)


def _bwd_block(t):
    return N_SSD_STEPS - 1 - t


def _ssd_tri():
    lower = np.tril(np.ones((CHUNK, CHUNK), np.float32))
    return jnp.asarray(np.stack([lower, lower.T]))


def _ssd_call(l, direction, p, cum, tr, w_mlp, extra=None):
    block_of = _fwd_block if direction == 0 else _bwd_block
    w_rows, w_cols = w_mlp.shape[1:]
    cast_steps = N_SSD_STEPS - 1
    slab = w_rows // cast_steps
    slab_of = lambda t: jnp.minimum(t, cast_steps - 1)
    rows = lambda width, cblk: pl.BlockSpec((STEP_CHUNKS * CHUNK, width), lambda t: (block_of(t), cblk))
    layer = lambda width: pl.BlockSpec((1, 1, width), lambda t: (l, 0, 0))
    bc_width = 2 * SSD_GROUPS * D_STATE
    in_specs = [rows(D_SSD, X_COL // D_SSD), rows(bc_width, BC_COL // bc_width), rows(DT_PAD, 0),
                pl.BlockSpec((STEP_CHUNKS, 2, DT_PAD, CHUNK), lambda t: (block_of(t), 0, 0, 0))]
    args = [p, p, cum, tr]
    if direction == 0:
        out_dtype = F32
    else:
        yf, dsk, gn = extra
        in_specs += [rows(D_SSD, 0), rows(D_SSD, Z_COL // D_SSD), layer(D_SSD), layer(D_SSD)]
        args += [yf, p, dsk, gn]
        out_dtype = BF16
    in_specs.append(pl.BlockSpec((1, slab, w_cols), lambda t: (l, slab_of(t), 0)))
    args.append(w_mlp)
    return pl.pallas_call(
        functools.partial(_ssd_kernel, direction),
        grid=(N_SSD_STEPS,),
        in_specs=in_specs,
        out_specs=[rows(D_SSD, 0), pl.BlockSpec((slab, w_cols), lambda t: (slab_of(t), 0))],
        out_shape=[jax.ShapeDtypeStruct((T_ALL, D_SSD), out_dtype),
                   jax.ShapeDtypeStruct((w_rows, w_cols), BF16)],
        scratch_shapes=[pltpu.VMEM((SSD_GROUPS * PAIRS_PER_GROUP, D_STATE, LANES), F32)],
        compiler_params=pltpu.CompilerParams(
            dimension_semantics=("arbitrary",), vmem_limit_bytes=VMEM_LIMIT),
        name="ssd_fwd" if direction == 0 else "ssd_bwd",
    )(*args)


def _outproj_kernel(split, *refs):
    if split:
        x_ref, ctx_ref, f_ref, y_ref, w_ref, gt_ref, o_ref, w16_scr = refs
    else:
        x_ref, f_ref, y_ref, w_ref, gt_ref, o_ref, w16_scr = refs
    i = pl.program_id(0)

    @pl.when(i == 0)
    def _():
        rows = 256

        def body(c, carry):
            rs = pl.ds(pl.multiple_of(c * rows, rows), rows)
            w16_scr[rs, :] = w_ref[0, rs, :].astype(BF16)
            return carry

        lax.fori_loop(0, D_MODEL // rows, body, 0)

    acc = (jnp.dot(f_ref[...], w16_scr[:D_FOURIER, :], preferred_element_type=F32)
           + jnp.dot(y_ref[...], w16_scr[D_FOURIER:, :], preferred_element_type=F32))
    if not split:
        gate = _pick(_is_ctx_rows(i, OUT_ROWS), gt_ref)
        o_ref[...] = x_ref[...] + gate * acc
    else:
        last = pl.num_programs(0) - 1
        n_lat = SEQ - (T_ALL // OUT_ROWS - 1) * OUT_ROWS

        @pl.when(i < last)
        def _():
            o_ref[...] = x_ref[...] + gt_ref[0:1, :] * acc

        @pl.when(i == last)
        def _():
            o_ref[:n_lat, :] = x_ref[:n_lat, :] + gt_ref[0:1, :] * acc[:n_lat]
            o_ref[n_lat:, :] = ctx_ref[...] + gt_ref[1:2, :] * acc[n_lat:]


def _outproj_call(l, xs, f, y, w_out, mod):
    rows = lambda width: pl.BlockSpec((OUT_ROWS, width), lambda i: (i, 0))
    stream_specs, stream_args = _stream_specs(xs, OUT_ROWS, 1)
    return pl.pallas_call(
        functools.partial(_outproj_kernel, isinstance(xs, tuple)),
        grid=(T_ALL // OUT_ROWS,),
        in_specs=stream_specs + [
            rows(D_FOURIER), rows(D_SSD),
            pl.BlockSpec((1, D_MODEL, D_MODEL), lambda i: (l, 0, 0), pipeline_mode=pl.Buffered(1)),
            pl.BlockSpec((8, D_MODEL), lambda i: (0, 2)),
        ],
        out_specs=rows(D_MODEL),
        out_shape=jax.ShapeDtypeStruct((T_ALL, D_MODEL), F32),
        scratch_shapes=[pltpu.VMEM((D_MODEL, D_MODEL), BF16)],
        compiler_params=pltpu.CompilerParams(
            dimension_semantics=("arbitrary",), vmem_limit_bytes=VMEM_LIMIT),
        name="out_proj",
    )(*stream_args, f, y, w_out, mod)


def _mlp_kernel(final, x_ref, g_ref, sh_ref, sc_ref, gt_ref, w1_ref, w2_ref, gf_ref, o_ref, xn_scr):
    i = pl.program_id(0)
    j = pl.program_id(1)

    @pl.when(j == 0)
    def _():
        _norm_modulate_rows(x_ref, None, g_ref, sh_ref, sc_ref, xn_scr, i, MLP_ROWS)
        o_ref[...] = jnp.zeros_like(o_ref)

    h = jnp.dot(xn_scr[...], w1_ref[...], preferred_element_type=F32)
    h = jnp.square(jnp.maximum(h, 0.0)).astype(BF16)
    o_ref[...] += jnp.dot(h, w2_ref[...], preferred_element_type=F32)

    @pl.when(j == pl.num_programs(1) - 1)
    def _():
        gate = _pick(_is_ctx_rows(i, MLP_ROWS), gt_ref)
        y = x_ref[...] + gate * o_ref[...]
        if final:
            y = y * lax.rsqrt(jnp.mean(y * y, axis=-1, keepdims=True) + EPS) * gf_ref[...]
        o_ref[...] = y


def _mlp_call(l, xs, g_mlp, mod, w1, w2, g_final, final):
    return pl.pallas_call(
        functools.partial(_mlp_kernel, final),
        grid=(T_ALL // MLP_ROWS, D_FF // FF_TILE),
        in_specs=[
            pl.BlockSpec((MLP_ROWS, D_MODEL), lambda i, j: (i, 0)),
            pl.BlockSpec((1, 1, D_MODEL), lambda i, j: (l, 0, 0)),
            pl.BlockSpec((8, D_MODEL), lambda i, j: (0, 3)),
            pl.BlockSpec((8, D_MODEL), lambda i, j: (0, 4)),
            pl.BlockSpec((8, D_MODEL), lambda i, j: (0, 5)),
            pl.BlockSpec((D_MODEL, FF_TILE), lambda i, j: (0, j)),
            pl.BlockSpec((FF_TILE, D_MODEL), lambda i, j: (j, 0)),
            pl.BlockSpec((1, D_MODEL), lambda i, j: (0, 0)),
        ],
        out_specs=pl.BlockSpec((MLP_ROWS, D_MODEL), lambda i, j: (i, 0)),
        out_shape=jax.ShapeDtypeStruct((SEQ if final else T_ALL, D_MODEL), F32),
        scratch_shapes=[pltpu.VMEM((MLP_ROWS, D_MODEL), BF16)],
        compiler_params=pltpu.CompilerParams(
            dimension_semantics=("arbitrary", "arbitrary"), vmem_limit_bytes=VMEM_LIMIT),
        name="mlp",
    )(xs, g_mlp, mod, mod, mod, w1, w2, g_final)


def _mixer_layer(l, xs, mod, tabs, tri, g_mix, w_in, conv_w, conv_b, dtb, alog, dsk, g_ssd_norm,
                 w_fourier, w_out, w_mlp1, w_mlp2):
    p, dtr = _inproj_call(l, xs, g_mix, mod, w_in, conv_w, conv_b)
    f = _fourier_lat_call(l, p, w_fourier, tabs)
    f = _fourier_ctx_call(l, p, w_fourier, tabs, f)
    cum, tr = _ssd_prep_call(l, dtr, dtb, alog, tri)
    yf, w1 = _ssd_call(l, 0, p, cum, tr, w_mlp1)
    y, w2 = _ssd_call(l, 1, p, cum, tr, w_mlp2, extra=(yf, dsk, g_ssd_norm))
    return _outproj_call(l, xs, f, y, w_out, mod), w1, w2


def kernel(x, c, ctx, c_ctx, w_ada, b_ada, g_mix, w_in, conv_w, conv_b, dt_bias, a_log, d_skip,
           g_ssd_norm, w_fourier, w_out, g_mlp, w_mlp1, w_mlp2, g_final):
    assert x.shape == (1, SEQ, D_MODEL) and ctx.shape == (1, CTX_LEN, D_MODEL)
    tabs = _dft_tables()
    tri = _ssd_tri()

    xs = (x[0], ctx[0])
    cc = jnp.concatenate([c, c_ctx[None, :], jnp.zeros((ADA_ROWS - 2, D_MODEL), F32)], axis=0)
    mods = _ada_call(cc, w_ada, b_ada)

    pad_heads = ((0, 0), (0, 0), (0, DT_PAD - 2 * SSD_HEADS))
    dtb = jnp.pad(dt_bias.reshape(DEPTH, 1, 2 * SSD_HEADS), pad_heads)
    alog = jnp.pad(a_log.reshape(DEPTH, 1, 2 * SSD_HEADS), pad_heads)
    dsk = jnp.repeat(d_skip, SSD_HEAD_DIM, axis=1).reshape(DEPTH, 1, D_SSD)
    row3 = lambda a: a.reshape(DEPTH, 1, a.shape[-1])
    w_in16 = w_in.astype(BF16)

    for l in range(DEPTH):
        xs, w1, w2 = _mixer_layer(l, xs, mods[l], tabs, tri, row3(g_mix), w_in16, conv_w, row3(conv_b),
                                  dtb, alog, dsk, row3(g_ssd_norm), w_fourier, w_out, w_mlp1, w_mlp2)
        xs = _mlp_call(l, xs, row3(g_mlp), mods[l], w1, w2, g_final[None, :], final=l == DEPTH - 1)

    return xs[None]
```

```python
import functools
import math

import numpy as np
import jax
import jax.numpy as jnp
from jax import lax
from jax.experimental import pallas as pl
from jax.experimental.pallas import tpu as pltpu

F32 = jnp.float32
BF16 = jnp.bfloat16
HIGHEST = lax.Precision.HIGHEST

D_MODEL = 2048
SEQ = 8192
CTX_LEN = 256
T_ALL = SEQ + CTX_LEN
DEPTH = 4
GRID_W = 64
D_FOURIER = 512
N_FGROUPS = 4
FG_DIM = 128
D_SSD = 1536
SSD_HEAD_DIM = 64
SSD_HEADS = 24
SSD_GROUPS = 4
HEADS_PER_GROUP = 6
PAIRS_PER_GROUP = HEADS_PER_GROUP // 2
GROUP_DIM = D_SSD // SSD_GROUPS
D_STATE = 128
D_CONV = 5
CONV_DIM = 2560
CHUNK = 128
N_CHUNKS = T_ALL // CHUNK
N_LAT_CHUNKS = SEQ // CHUNK
STEP_CHUNKS = CTX_LEN // CHUNK
N_SSD_STEPS = N_CHUNKS // STEP_CHUNKS
D_MAIN = D_FOURIER + D_SSD + CONV_DIM
D_IN_PROJ = D_MAIN + 2 * SSD_HEADS
DT_PAD = 128
D_FF = 4 * D_MODEL
EPS = 1e-6
LOG2_E = float(np.log2(np.e))

SUBLANES = 8
LANES = 128
COL_TILE = 512
FF_TILE = 1024
MLP_ROWS = 768
PROJ_ROWS = 1408
OUT_ROWS = 528
ADA_ROWS = 16
ROW_CHUNK = 128
PREP_CHUNKS = 6
FFT_N1 = 64
FFT_N2 = 128
VMEM_LIMIT = 56 * 1024 * 1024

N_PROJ_TILES = D_MAIN // COL_TILE
N_PROJ_ROW_BLOCKS = T_ALL // PROJ_ROWS
N_PROJ_STEPS = N_PROJ_ROW_BLOCKS * N_PROJ_TILES
FIRST_XBC_TILE = (D_FOURIER + D_SSD) // COL_TILE
Z_COL = 0
X_COL = D_SSD
BC_COL = 2 * D_SSD
F_COL = D_SSD + CONV_DIM


def _silu(v):
    return v * jax.nn.sigmoid(v)


def _is_ctx_rows(block_idx, rows_per_block):
    row = block_idx * rows_per_block + lax.broadcasted_iota(jnp.int32, (rows_per_block, 1), 0)
    return row >= SEQ


def _pick(is_ctx, ref):
    return jnp.where(is_ctx, ref[1:2, :], ref[0:1, :])


def _norm_modulate_rows(x_ref, ctx_ref, g_ref, sh_ref, sc_ref, xn_scr, block_idx, rows_per_block):
    g = g_ref[0]
    chunk = math.gcd(rows_per_block, ROW_CHUNK)
    assert SEQ % chunk == 0

    def body(c, carry):
        r0 = pl.multiple_of(c * chunk, chunk)
        row = block_idx * rows_per_block + r0
        is_ctx = row >= SEQ
        if ctx_ref is None:
            x = x_ref[pl.ds(r0, chunk), :]
        else:
            x = lax.cond(is_ctx,
                         lambda: ctx_ref[pl.ds(pl.multiple_of(row - SEQ, chunk), chunk), :],
                         lambda: x_ref[pl.ds(r0, chunk), :])
        y = x * lax.rsqrt(jnp.mean(x * x, axis=-1, keepdims=True) + EPS) * g
        y = y * (1.0 + _pick(is_ctx, sc_ref)) + _pick(is_ctx, sh_ref)
        xn_scr[pl.ds(r0, chunk), :] = y.astype(BF16)
        return carry

    lax.fori_loop(0, rows_per_block // chunk, body, 0)


def _ada_kernel(c_ref, w_ref, b_ref, o_ref):
    s = _silu(c_ref[...]).astype(BF16)
    o_ref[0] = jnp.dot(s, w_ref[0].astype(BF16), preferred_element_type=F32) + b_ref[0]


def _ada_call(cc, w_ada, b_ada):
    tn = 1024
    return pl.pallas_call(
        _ada_kernel,
        grid=(DEPTH, 6 * D_MODEL // tn),
        in_specs=[
            pl.BlockSpec((ADA_ROWS, D_MODEL), lambda l, j: (0, 0)),
            pl.BlockSpec((1, D_MODEL, tn), lambda l, j: (l, 0, j)),
            pl.BlockSpec((1, 1, tn), lambda l, j: (l, 0, j)),
        ],
        out_specs=pl.BlockSpec((1, ADA_ROWS, tn), lambda l, j: (l, 0, j)),
        out_shape=jax.ShapeDtypeStruct((DEPTH, ADA_ROWS, 6 * D_MODEL), F32),
        compiler_params=pltpu.CompilerParams(
            dimension_semantics=("arbitrary", "arbitrary"), vmem_limit_bytes=VMEM_LIMIT),
        name="ada_mod",
    )(cc, w_ada, b_ada.reshape(DEPTH, 1, 6 * D_MODEL))


def _conv_silu(v, w_ref, b_ref, group_rows, col_slice):
    rows, cols = v.shape
    tiles = group_rows // SUBLANES
    v4 = v.reshape(rows // group_rows, tiles, SUBLANES, cols)
    sub = lax.broadcasted_iota(jnp.int32, (1, 1, SUBLANES, cols), 2)
    zero_tile = jnp.zeros((rows // group_rows, 1, SUBLANES, cols), F32)
    half = D_CONV // 2
    w_ref = w_ref.at[:, :, col_slice]
    b_ref = b_ref.at[:, :, col_slice]
    acc = v4 * w_ref[0, half:half + 1, :]
    for k in range(D_CONV):
        off = k - half
        if off == 0:
            continue
        r = pltpu.roll(v4, (-off) % SUBLANES, axis=2)
        if off < 0:
            nbr = jnp.concatenate([zero_tile, r[:, :-1]], axis=1)
            shifted = jnp.where(sub < -off, nbr, r)
        else:
            nbr = jnp.concatenate([r[:, 1:], zero_tile], axis=1)
            shifted = jnp.where(sub < SUBLANES - off, r, nbr)
        acc = acc + shifted * w_ref[0, k:k + 1, :]
    return _silu(acc + b_ref[0]).reshape(rows, cols)


def _inproj_kernel(split, *refs):
    if split:
        (x_ref, ctx_ref, g_ref, sh_ref, sc_ref, w_ref, wdt_ref, cw_ref, cb_ref, wsrc_ref,
         p_ref, dt_ref, wdst_ref, xn_scr) = refs
    else:
        (x_ref, g_ref, sh_ref, sc_ref, w_ref, wdt_ref, cw_ref, cb_ref, wsrc_ref,
         p_ref, dt_ref, wdst_ref, xn_scr) = refs
        ctx_ref = None
    i = pl.program_id(0)
    j = pl.program_id(1)
    last = pl.num_programs(0) - 1
    wdst_ref[...] = wsrc_ref[0].astype(BF16)

    @pl.when(j == 0)
    def _():
        _norm_modulate_rows(x_ref, ctx_ref, g_ref, sh_ref, sc_ref, xn_scr, i, PROJ_ROWS)
        lane = lax.broadcasted_iota(jnp.int32, (1, DT_PAD), 1)
        wdt = jnp.where(lane < 2 * SSD_HEADS, wdt_ref[0], jnp.zeros((), BF16))
        dt_ref[...] = jnp.dot(xn_scr[...], wdt, preferred_element_type=F32)

    def matmul():
        return jnp.dot(xn_scr[...], w_ref[0], preferred_element_type=F32)

    all_cols = slice(0, COL_TILE)

    @pl.when(j < FIRST_XBC_TILE)
    def _():
        p_ref[...] = matmul()

    @pl.when(jnp.logical_and(j >= FIRST_XBC_TILE, i < last))
    def _():
        bounds = (0, 384, 768, 1088, PROJ_ROWS)
        for r0, r1 in zip(bounds[:-1], bounds[1:]):
            acc = jnp.dot(xn_scr[r0:r1, :], w_ref[0], preferred_element_type=F32)
            p_ref[r0:r1, :] = _conv_silu(acc, cw_ref, cb_ref, GRID_W, all_cols)

    @pl.when(jnp.logical_and(j >= FIRST_XBC_TILE, i == last))
    def _():
        n_lat = PROJ_ROWS - CTX_LEN
        acc = matmul()
        p_ref[:n_lat, :] = _conv_silu(acc[:n_lat], cw_ref, cb_ref, GRID_W, all_cols)
        p_ref[n_lat:, :] = _conv_silu(acc[n_lat:], cw_ref, cb_ref, CTX_LEN, all_cols)


def _proj_col_block(j):
    return jnp.where(j == 0, N_PROJ_TILES - 1, j - 1)


def _stream_specs(stream, rows, n_grid_axes):
    first = (lambda i: (i, 0)) if n_grid_axes == 1 else (lambda i, j: (i, 0))
    zero = (lambda i: (0, 0)) if n_grid_axes == 1 else (lambda i, j: (0, 0))
    if isinstance(stream, tuple):
        return [pl.BlockSpec((rows, D_MODEL), first), pl.BlockSpec((CTX_LEN, D_MODEL), zero)], list(stream)
    return [pl.BlockSpec((rows, D_MODEL), first)], [stream]


def _inproj_call(l, xs, g_mix, mod, w_in, conv_w, conv_b, w_mlp1):
    xbc_tile = lambda j: jnp.maximum(j - FIRST_XBC_TILE, 0)
    stream_specs, stream_args = _stream_specs(xs, PROJ_ROWS, 2)
    cast_steps = 32
    slab = D_MODEL // cast_steps
    slab_of = lambda i, j: jnp.minimum(i * N_PROJ_TILES + j, cast_steps - 1)
    return pl.pallas_call(
        functools.partial(_inproj_kernel, isinstance(xs, tuple)),
        grid=(N_PROJ_ROW_BLOCKS, N_PROJ_TILES),
        in_specs=stream_specs + [
            pl.BlockSpec((1, 1, D_MODEL), lambda i, j: (l, 0, 0)),
            pl.BlockSpec((8, D_MODEL), lambda i, j: (0, 0)),
            pl.BlockSpec((8, D_MODEL), lambda i, j: (0, 1)),
            pl.BlockSpec((1, D_MODEL, COL_TILE), lambda i, j: (l, 0, j)),
            pl.BlockSpec((1, D_MODEL, DT_PAD), lambda i, j: (l, 0, D_MAIN // DT_PAD)),
            pl.BlockSpec((1, D_CONV, COL_TILE), lambda i, j: (l, 0, xbc_tile(j))),
            pl.BlockSpec((1, 1, COL_TILE), lambda i, j: (l, 0, xbc_tile(j))),
            pl.BlockSpec((1, slab, D_FF), lambda i, j: (l, slab_of(i, j), 0)),
        ],
        out_specs=[
            pl.BlockSpec((PROJ_ROWS, COL_TILE), lambda i, j: (i, _proj_col_block(j))),
            pl.BlockSpec((PROJ_ROWS, DT_PAD), lambda i, j: (i, 0)),
            pl.BlockSpec((slab, D_FF), lambda i, j: (slab_of(i, j), 0)),
        ],
        out_shape=[
            jax.ShapeDtypeStruct((T_ALL, D_MAIN), F32),
            jax.ShapeDtypeStruct((T_ALL, DT_PAD), F32),
            jax.ShapeDtypeStruct((D_MODEL, D_FF), BF16),
        ],
        scratch_shapes=[pltpu.VMEM((PROJ_ROWS, D_MODEL), BF16)],
        compiler_params=pltpu.CompilerParams(
            dimension_semantics=("arbitrary", "arbitrary"), vmem_limit_bytes=VMEM_LIMIT),
        name="in_proj",
    )(*stream_args, g_mix, mod, mod, w_in, w_in, conv_w, conv_b, w_mlp1)


def _dft_tables():
    L, n1, n2 = SEQ, FFT_N1, FFT_N2
    a = np.arange(n1)[:, None, None]
    k2 = np.arange(n2)[None, :, None]
    b = np.arange(n2)[None, None, :]
    ang = 2.0 * np.pi * ((k2 * (a + n1 * b)) % L) / L
    t1 = np.concatenate([np.cos(ang), -np.sin(ang)], axis=1)
    k1 = np.arange(n1)[:, None]
    aa = np.arange(n1)[None, :]
    ang2 = 2.0 * np.pi * ((k1 * aa) % n1) / n1
    c2, s2 = np.cos(ang2), np.sin(ang2)
    f2 = np.block([[c2, s2], [-s2, c2]])
    cc = np.arange(FG_DIM)
    angc = 2.0 * np.pi * ((cc[:, None] * cc[None, :]) % FG_DIM) / FG_DIM
    lc = np.arange(CTX_LEN)
    angl = 2.0 * np.pi * ((lc[:, None] * lc[None, :]) % CTX_LEN) / CTX_LEN
    as32 = lambda v: jnp.asarray(v, dtype=F32)
    as16 = lambda v: jnp.asarray(v, dtype=F32).astype(BF16)
    return dict(t1=as16(t1), f2=as16(f2), cc=as32(np.cos(angc)), sc=as32(np.sin(angc)),
                cl=as32(np.cos(angl)), sl=as32(np.sin(angl)))


def _fourier_lat_kernel(u_ref, t1_ref, f2_ref, cc_ref, sc_ref, w_ref, o_ref,
                        zr_scr, zi_scr, xr_scr, xi_scr):
    n1, n2 = FFT_N1, FFT_N2
    scale = 1.0 / np.sqrt(float(SEQ) * FG_DIM)
    w = w_ref[0, 0]
    g1 = jnp.dot(cc_ref[...], w, precision=HIGHEST, preferred_element_type=F32) * scale
    g2 = jnp.dot(sc_ref[...], w, precision=HIGHEST, preferred_element_type=F32) * scale
    gmat = jnp.concatenate([g1, g2], axis=0).astype(BF16)

    batch = SUBLANES

    def stage1(a, carry):
        xa = u_ref[pl.ds(a, n2, stride=n1), :].astype(BF16)
        z = jnp.dot(t1_ref[a], xa, preferred_element_type=F32)
        zr_scr[a] = z[:n2]
        zi_scr[a] = z[n2:]
        return carry

    lax.fori_loop(0, n1, stage1, 0, unroll=8)

    def stage2(kb, carry):
        k2 = pl.multiple_of(kb * batch, batch)
        zr = jnp.swapaxes(zr_scr[:, pl.ds(k2, batch), :], 0, 1)
        zi = jnp.swapaxes(zi_scr[:, pl.ds(k2, batch), :], 0, 1)
        cols = [jnp.concatenate([zr[q], zi[q]], axis=0).astype(BF16) for q in range(batch)]
        rhs = jnp.concatenate(cols, axis=1)
        res = jnp.dot(f2_ref[...], rhs, preferred_element_type=F32)
        for q in range(batch):
            blk = res[:, q * FG_DIM:(q + 1) * FG_DIM]
            xr_scr[k2 + q] = blk[:n1]
            xi_scr[k2 + q] = blk[n1:]
        return carry

    lax.fori_loop(0, n2 // batch, stage2, 0, unroll=2)

    def finish(kb, carry):
        k1 = pl.multiple_of(kb * batch, batch)
        xr = jnp.swapaxes(xr_scr[:, pl.ds(k1, batch), :], 0, 1)
        xi = jnp.swapaxes(xi_scr[:, pl.ds(k1, batch), :], 0, 1)
        for q in range(batch):
            xri = jnp.concatenate([xr[q], xi[q]], axis=1).astype(BF16)
            o = jnp.dot(xri, gmat, preferred_element_type=F32)
            row0 = pl.multiple_of((k1 + q) * n2, n2)
            o_ref[pl.ds(row0, n2), :] = o.astype(o_ref.dtype)
        return carry

    lax.fori_loop(0, n1 // batch, finish, 0)


def _fourier_lat_call(l, p, w_f, tabs):
    col0 = F_COL // FG_DIM
    const = lambda shape: pl.BlockSpec(shape, lambda g: (0,) * len(shape))
    return pl.pallas_call(
        _fourier_lat_kernel,
        grid=(N_FGROUPS,),
        in_specs=[
            pl.BlockSpec((SEQ, FG_DIM), lambda g: (0, col0 + g)),
            const((FFT_N1, 2 * FFT_N2, FFT_N2)),
            const((2 * FFT_N1, 2 * FFT_N1)),
            const((FG_DIM, FG_DIM)),
            const((FG_DIM, FG_DIM)),
            pl.BlockSpec((1, 1, FG_DIM, FG_DIM), lambda g: (l, g, 0, 0)),
        ],
        out_specs=pl.BlockSpec((SEQ, FG_DIM), lambda g: (0, g)),
        out_shape=jax.ShapeDtypeStruct((T_ALL, D_FOURIER), BF16),
        scratch_shapes=([pltpu.VMEM((FFT_N1, FFT_N2, FG_DIM), F32)] * 2
                        + [pltpu.VMEM((FFT_N2, FFT_N1, FG_DIM), F32)] * 2),
        compiler_params=pltpu.CompilerParams(
            dimension_semantics=("arbitrary",), vmem_limit_bytes=VMEM_LIMIT),
        name="fourier_lat",
    )(p, tabs["t1"], tabs["f2"], tabs["cc"], tabs["sc"], w_f)


def _fourier_ctx_kernel(u_ref, cl_ref, sl_ref, cc_ref, sc_ref, w_ref, f_hbm_ref, o_ref):
    del f_hbm_ref
    scale = 1.0 / np.sqrt(float(CTX_LEN) * FG_DIM)
    dot = functools.partial(jnp.dot, precision=HIGHEST, preferred_element_type=F32)
    w = w_ref[0, 0]
    u = u_ref[...]
    a = dot(u, dot(cc_ref[...], w))
    b = dot(u, dot(sc_ref[...], w))
    o_ref[...] = ((dot(cl_ref[...], a) - dot(sl_ref[...], b)) * scale).astype(o_ref.dtype)


def _fourier_ctx_call(l, p, w_f, tabs, f):
    col0 = F_COL // FG_DIM
    row_blk = SEQ // CTX_LEN
    const = lambda shape: pl.BlockSpec(shape, lambda g: (0,) * len(shape))
    return pl.pallas_call(
        _fourier_ctx_kernel,
        grid=(N_FGROUPS,),
        in_specs=[
            pl.BlockSpec((CTX_LEN, FG_DIM), lambda g: (row_blk, col0 + g)),
            const((CTX_LEN, CTX_LEN)),
            const((CTX_LEN, CTX_LEN)),
            const((FG_DIM, FG_DIM)),
            const((FG_DIM, FG_DIM)),
            pl.BlockSpec((1, 1, FG_DIM, FG_DIM), lambda g: (l, g, 0, 0)),
            pl.BlockSpec(memory_space=pl.ANY),
        ],
        out_specs=pl.BlockSpec((CTX_LEN, FG_DIM), lambda g: (row_blk, g)),
        out_shape=jax.ShapeDtypeStruct((T_ALL, D_FOURIER), BF16),
        input_output_aliases={6: 0},
        compiler_params=pltpu.CompilerParams(
            dimension_semantics=("arbitrary",), vmem_limit_bytes=VMEM_LIMIT),
        name="fourier_ctx",
    )(p, tabs["cl"], tabs["sl"], tabs["cc"], tabs["sc"], w_f, f)


def _ssd_prep_kernel(dtr_ref, dtb_ref, alog_ref, tri_ref, cum_ref, tr_ref):
    q = CHUNK
    is_fwd = lax.broadcasted_iota(jnp.int32, (1, DT_PAD), 1) < SSD_HEADS
    neg_a = -jnp.exp(alog_ref[0])
    for c in range(PREP_CHUNKS):
        rs = slice(c * q, (c + 1) * q)
        v = dtr_ref[rs, :] + dtb_ref[0]
        dt = jnp.maximum(v, 0.0) + jnp.log1p(jnp.exp(-jnp.abs(v)))
        dta = dt * neg_a
        run = jnp.dot(tri_ref[0], dta, precision=HIGHEST, preferred_element_type=F32)
        rev = jnp.dot(tri_ref[1], dta, precision=HIGHEST, preferred_element_type=F32)
        cum = jnp.where(is_fwd, run, rev)
        total = jnp.where(is_fwd, run[q - 1:q, :], rev[0:1, :])
        cum2 = cum * LOG2_E
        cum_ref[rs, :] = cum2
        tr_ref[c, 0] = (cum2 - jnp.log2(dt)).T
        tr_ref[c, 1] = (dt * jnp.exp(total - cum)).T


def _ssd_prep_call(l, dtr, dtb, alog, tri):
    layer = lambda width: pl.BlockSpec((1, 1, width), lambda t: (l, 0, 0))
    rows = PREP_CHUNKS * CHUNK
    return pl.pallas_call(
        _ssd_prep_kernel,
        grid=(N_CHUNKS // PREP_CHUNKS,),
        in_specs=[pl.BlockSpec((rows, DT_PAD), lambda t: (t, 0)), layer(DT_PAD), layer(DT_PAD),
                  pl.BlockSpec((2, CHUNK, CHUNK), lambda t: (0, 0, 0))],
        out_specs=[pl.BlockSpec((rows, DT_PAD), lambda t: (t, 0)),
                   pl.BlockSpec((PREP_CHUNKS, 2, DT_PAD, CHUNK), lambda t: (t, 0, 0, 0))],
        out_shape=[jax.ShapeDtypeStruct((T_ALL, DT_PAD), F32),
                   jax.ShapeDtypeStruct((N_CHUNKS, 2, DT_PAD, CHUNK), F32)],
        compiler_params=pltpu.CompilerParams(
            dimension_semantics=("arbitrary",), vmem_limit_bytes=VMEM_LIMIT),
        name="ssd_prep",
    )(dtr, dtb, alog, tri)


def _ssd_kernel(direction, *refs):
    forward = direction == 0
    if forward:
        x_ref, bc_ref, cum_ref, tr_ref, o_ref, h_scr = refs
    else:
        (x_ref, bc_ref, cum_ref, tr_ref, yf_ref, z_ref, dsk_ref, gn_ref, wsrc_ref,
         o_ref, wdst_ref, h_scr) = refs
        wdst_ref[...] = wsrc_ref[0].astype(BF16)
    q = CHUNK
    edge = q - 1 if forward else 0

    @pl.when(pl.program_id(0) == 0)
    def _():
        h_scr[...] = jnp.zeros_like(h_scr)

    row = lax.broadcasted_iota(jnp.int32, (q, q), 0)
    col = lax.broadcasted_iota(jnp.int32, (q, q), 1)
    keep = (col <= row) if forward else (col >= row)
    low_lanes = col < SSD_HEAD_DIM

    for c in (range(STEP_CHUNKS) if forward else reversed(range(STEP_CHUNKS))):
        rs = slice(c * q, (c + 1) * q)
        cum2 = cum_ref[rs, :]
        cdt_t = tr_ref[c, 0]
        w_t = tr_ref[c, 1]
        x = x_ref[rs, :]
        x16 = x.astype(BF16)

        ys = []
        for g in range(SSD_GROUPS):
            bm = bc_ref[rs, g * D_STATE:(g + 1) * D_STATE]
            cm = bc_ref[rs, (SSD_GROUPS + g) * D_STATE:(SSD_GROUPS + g + 1) * D_STATE]
            bt = bm.T
            c16 = cm.astype(BF16)
            cb = jnp.dot(c16, bt.astype(BF16), preferred_element_type=F32)
            h_prev = [h_scr[PAIRS_PER_GROUP * g + j] for j in range(PAIRS_PER_GROUP)]
            y_off = jnp.dot(c16, jnp.concatenate(h_prev, axis=1).astype(BF16),
                            preferred_element_type=F32)
            for j in range(PAIRS_PER_GROUP):
                top, bot, ecol = [], [], []
                for r in range(2):
                    hh = SSD_HEADS * direction + HEADS_PER_GROUP * g + 2 * j + r
                    colb = jnp.broadcast_to(cum2[:, hh:hh + 1], (q, q))
                    seg = colb - cdt_t[hh:hh + 1, :]
                    decay_dt = jnp.exp2(jnp.where(keep, seg, -jnp.inf))
                    top.append((decay_dt * cb).astype(BF16))
                    bot.append((bt * w_t[hh:hh + 1, :]).astype(BF16))
                    ecol.append(jnp.exp2(colb))
                lhs = jnp.concatenate([jnp.concatenate(top, axis=1),
                                       jnp.concatenate(bot, axis=1)], axis=0)
                c0 = g * GROUP_DIM + j * LANES
                xp = x16[:, c0:c0 + LANES]
                zero = jnp.zeros_like(xp)
                rhs = jnp.concatenate([jnp.where(low_lanes, xp, zero),
                                       jnp.where(low_lanes, zero, xp)], axis=0)
                res = jnp.dot(lhs, rhs, preferred_element_type=F32)
                escale = jnp.where(low_lanes, ecol[0], ecol[1])
                ys.append(res[:q] + y_off[:, j * LANES:(j + 1) * LANES] * escale)
                h_scr[PAIRS_PER_GROUP * g + j] = h_prev[j] * escale[edge:edge + 1, :] + res[q:]
        y = jnp.concatenate(ys, axis=1)

        if forward:
            o_ref[rs, :] = y
        else:
            y = yf_ref[rs, :] + y + dsk_ref[0] * x
            u = y * _silu(z_ref[rs, :])
            outs = []
            for g in range(SSD_GROUPS):
                ug = u[:, g * GROUP_DIM:(g + 1) * GROUP_DIM]
                outs.append(ug * lax.rsqrt(jnp.mean(ug * ug, axis=-1, keepdims=True) + EPS))
            o_ref[rs, :] = (jnp.concatenate(outs, axis=1) * gn_ref[0]).astype(o_ref.dtype)


def _fwd_block(t):
    return jnp.where(t == 0, N_SSD_STEPS - 1, t - 1)


def _bwd_block(t):
    return N_SSD_STEPS - 1 - t


def _ssd_tri():
    lower = np.tril(np.ones((CHUNK, CHUNK), np.float32))
    return jnp.asarray(np.stack([lower, lower.T]))


def _ssd_call(l, direction, p, cum, tr, w_mlp=None, extra=None):
    block_of = _fwd_block if direction == 0 else _bwd_block
    rows = lambda width, cblk: pl.BlockSpec((STEP_CHUNKS * CHUNK, width), lambda t: (block_of(t), cblk))
    layer = lambda width: pl.BlockSpec((1, 1, width), lambda t: (l, 0, 0))
    bc_width = 2 * SSD_GROUPS * D_STATE
    in_specs = [rows(D_SSD, X_COL // D_SSD), rows(bc_width, BC_COL // bc_width), rows(DT_PAD, 0),
                pl.BlockSpec((STEP_CHUNKS, 2, DT_PAD, CHUNK), lambda t: (block_of(t), 0, 0, 0))]
    args = [p, p, cum, tr]
    if direction == 0:
        out_specs = rows(D_SSD, 0)
        out_shape = jax.ShapeDtypeStruct((T_ALL, D_SSD), F32)
    else:
        yf, dsk, gn = extra
        w_rows, w_cols = w_mlp.shape[1:]
        cast_steps = N_SSD_STEPS - 1
        slab = w_rows // cast_steps
        slab_of = lambda t: jnp.minimum(t, cast_steps - 1)
        in_specs += [rows(D_SSD, 0), rows(D_SSD, Z_COL // D_SSD), layer(D_SSD), layer(D_SSD),
                     pl.BlockSpec((1, slab, w_cols), lambda t: (l, slab_of(t), 0))]
        args += [yf, p, dsk, gn, w_mlp]
        out_specs = [rows(D_SSD, 0), pl.BlockSpec((slab, w_cols), lambda t: (slab_of(t), 0))]
        out_shape = [jax.ShapeDtypeStruct((T_ALL, D_SSD), BF16),
                     jax.ShapeDtypeStruct((w_rows, w_cols), BF16)]
    return pl.pallas_call(
        functools.partial(_ssd_kernel, direction),
        grid=(N_SSD_STEPS,),
        in_specs=in_specs,
        out_specs=out_specs,
        out_shape=out_shape,
        scratch_shapes=[pltpu.VMEM((SSD_GROUPS * PAIRS_PER_GROUP, D_STATE, LANES), F32)],
        compiler_params=pltpu.CompilerParams(
            dimension_semantics=("arbitrary",), vmem_limit_bytes=VMEM_LIMIT),
        name="ssd_fwd" if direction == 0 else "ssd_bwd",
    )(*args)


def _outproj_kernel(split, *refs):
    if split:
        x_ref, ctx_ref, f_ref, y_ref, w_ref, gt_ref, o_ref, w16_scr = refs
    else:
        x_ref, f_ref, y_ref, w_ref, gt_ref, o_ref, w16_scr = refs
    i = pl.program_id(0)

    @pl.when(i == 0)
    def _():
        rows = 256

        def body(c, carry):
            rs = pl.ds(pl.multiple_of(c * rows, rows), rows)
            w16_scr[rs, :] = w_ref[0, rs, :].astype(BF16)
            return carry

        lax.fori_loop(0, D_MODEL // rows, body, 0)

    acc = (jnp.dot(f_ref[...], w16_scr[:D_FOURIER, :], preferred_element_type=F32)
           + jnp.dot(y_ref[...], w16_scr[D_FOURIER:, :], preferred_element_type=F32))
    if not split:
        gate = _pick(_is_ctx_rows(i, OUT_ROWS), gt_ref)
        o_ref[...] = x_ref[...] + gate * acc
    else:
        last = pl.num_programs(0) - 1
        n_lat = SEQ - (T_ALL // OUT_ROWS - 1) * OUT_ROWS

        @pl.when(i < last)
        def _():
            o_ref[...] = x_ref[...] + gt_ref[0:1, :] * acc

        @pl.when(i == last)
        def _():
            o_ref[:n_lat, :] = x_ref[:n_lat, :] + gt_ref[0:1, :] * acc[:n_lat]
            o_ref[n_lat:, :] = ctx_ref[...] + gt_ref[1:2, :] * acc[n_lat:]


def _outproj_call(l, xs, f, y, w_out, mod):
    rows = lambda width: pl.BlockSpec((OUT_ROWS, width), lambda i: (i, 0))
    stream_specs, stream_args = _stream_specs(xs, OUT_ROWS, 1)
    return pl.pallas_call(
        functools.partial(_outproj_kernel, isinstance(xs, tuple)),
        grid=(T_ALL // OUT_ROWS,),
        in_specs=stream_specs + [
            rows(D_FOURIER), rows(D_SSD),
            pl.BlockSpec((1, D_MODEL, D_MODEL), lambda i: (l, 0, 0), pipeline_mode=pl.Buffered(1)),
            pl.BlockSpec((8, D_MODEL), lambda i: (0, 2)),
        ],
        out_specs=rows(D_MODEL),
        out_shape=jax.ShapeDtypeStruct((T_ALL, D_MODEL), F32),
        scratch_shapes=[pltpu.VMEM((D_MODEL, D_MODEL), BF16)],
        compiler_params=pltpu.CompilerParams(
            dimension_semantics=("arbitrary",), vmem_limit_bytes=VMEM_LIMIT),
        name="out_proj",
    )(*stream_args, f, y, w_out, mod)


def _mlp_kernel(final, x_ref, g_ref, sh_ref, sc_ref, gt_ref, w1_ref, w2_ref, gf_ref, o_ref, xn_scr):
    i = pl.program_id(0)
    j = pl.program_id(1)

    @pl.when(j == 0)
    def _():
        _norm_modulate_rows(x_ref, None, g_ref, sh_ref, sc_ref, xn_scr, i, MLP_ROWS)
        o_ref[...] = jnp.zeros_like(o_ref)

    h = jnp.dot(xn_scr[...], w1_ref[...], preferred_element_type=F32)
    h = jnp.square(jnp.maximum(h, 0.0)).astype(BF16)
    o_ref[...] += jnp.dot(h, w2_ref[...], preferred_element_type=F32)

    @pl.when(j == pl.num_programs(1) - 1)
    def _():
        gate = _pick(_is_ctx_rows(i, MLP_ROWS), gt_ref)
        y = x_ref[...] + gate * o_ref[...]
        if final:
            y = y * lax.rsqrt(jnp.mean(y * y, axis=-1, keepdims=True) + EPS) * gf_ref[...]
        o_ref[...] = y


def _mlp_call(l, xs, g_mlp, mod, w1, w2, g_final, final):
    return pl.pallas_call(
        functools.partial(_mlp_kernel, final),
        grid=(T_ALL // MLP_ROWS, D_FF // FF_TILE),
        in_specs=[
            pl.BlockSpec((MLP_ROWS, D_MODEL), lambda i, j: (i, 0)),
            pl.BlockSpec((1, 1, D_MODEL), lambda i, j: (l, 0, 0)),
            pl.BlockSpec((8, D_MODEL), lambda i, j: (0, 3)),
            pl.BlockSpec((8, D_MODEL), lambda i, j: (0, 4)),
            pl.BlockSpec((8, D_MODEL), lambda i, j: (0, 5)),
            pl.BlockSpec((D_MODEL, FF_TILE), lambda i, j: (0, j)),
            pl.BlockSpec((FF_TILE, D_MODEL), lambda i, j: (j, 0)),
            pl.BlockSpec((1, D_MODEL), lambda i, j: (0, 0)),
        ],
        out_specs=pl.BlockSpec((MLP_ROWS, D_MODEL), lambda i, j: (i, 0)),
        out_shape=jax.ShapeDtypeStruct((SEQ if final else T_ALL, D_MODEL), F32),
        scratch_shapes=[pltpu.VMEM((MLP_ROWS, D_MODEL), BF16)],
        compiler_params=pltpu.CompilerParams(
            dimension_semantics=("arbitrary", "arbitrary"), vmem_limit_bytes=VMEM_LIMIT),
        name="mlp",
    )(xs, g_mlp, mod, mod, mod, w1, w2, g_final)


def _mixer_layer(l, xs, mod, tabs, tri, g_mix, w_in, conv_w, conv_b, dtb, alog, dsk, g_ssd_norm,
                 w_fourier, w_out, w_mlp1, w_mlp2):
    p, dtr, w1 = _inproj_call(l, xs, g_mix, mod, w_in, conv_w, conv_b, w_mlp1)
    f = _fourier_lat_call(l, p, w_fourier, tabs)
    f = _fourier_ctx_call(l, p, w_fourier, tabs, f)
    cum, tr = _ssd_prep_call(l, dtr, dtb, alog, tri)
    yf = _ssd_call(l, 0, p, cum, tr)
    y, w2 = _ssd_call(l, 1, p, cum, tr, w_mlp=w_mlp2, extra=(yf, dsk, g_ssd_norm))
    return _outproj_call(l, xs, f, y, w_out, mod), w1, w2


def kernel(x, c, ctx, c_ctx, w_ada, b_ada, g_mix, w_in, conv_w, conv_b, dt_bias, a_log, d_skip,
           g_ssd_norm, w_fourier, w_out, g_mlp, w_mlp1, w_mlp2, g_final):
    assert x.shape == (1, SEQ, D_MODEL) and ctx.shape == (1, CTX_LEN, D_MODEL)
    tabs = _dft_tables()
    tri = _ssd_tri()

    xs = (x[0], ctx[0])
    cc = jnp.concatenate([c, c_ctx[None, :], jnp.zeros((ADA_ROWS - 2, D_MODEL), F32)], axis=0)
    mods = _ada_call(cc, w_ada, b_ada)

    pad_heads = ((0, 0), (0, 0), (0, DT_PAD - 2 * SSD_HEADS))
    dtb = jnp.pad(dt_bias.reshape(DEPTH, 1, 2 * SSD_HEADS), pad_heads)
    alog = jnp.pad(a_log.reshape(DEPTH, 1, 2 * SSD_HEADS), pad_heads)
    dsk = jnp.repeat(d_skip, SSD_HEAD_DIM, axis=1).reshape(DEPTH, 1, D_SSD)
    row3 = lambda a: a.reshape(DEPTH, 1, a.shape[-1])
    w_in16 = w_in.astype(BF16)

    for l in range(DEPTH):
        xs, w1, w2 = _mixer_layer(l, xs, mods[l], tabs, tri, row3(g_mix), w_in16, conv_w, row3(conv_b),
                                  dtb, alog, dsk, row3(g_ssd_norm), w_fourier, w_out, w_mlp1, w_mlp2)
        xs = _mlp_call(l, xs, row3(g_mlp), mods[l], w1, w2, g_final[None, :], final=l == DEPTH - 1)

    return xs[None]
```

```python
import functools
import math

import numpy as np
import jax
import jax.numpy as jnp
from jax import lax
from jax.experimental import pallas as pl
from jax.experimental.pallas import tpu as pltpu

F32 = jnp.float32
BF16 = jnp.bfloat16
HIGHEST = lax.Precision.HIGHEST

D_MODEL = 2048
SEQ = 8192
CTX_LEN = 256
T_ALL = SEQ + CTX_LEN
DEPTH = 4
GRID_W = 64
D_FOURIER = 512
N_FGROUPS = 4
FG_DIM = 128
D_SSD = 1536
SSD_HEAD_DIM = 64
SSD_HEADS = 24
SSD_GROUPS = 4
HEADS_PER_GROUP = 6
PAIRS_PER_GROUP = HEADS_PER_GROUP // 2
GROUP_DIM = D_SSD // SSD_GROUPS
D_STATE = 128
D_CONV = 5
CONV_DIM = 2560
CHUNK = 128
N_CHUNKS = T_ALL // CHUNK
N_LAT_CHUNKS = SEQ // CHUNK
STEP_CHUNKS = CTX_LEN // CHUNK
N_SSD_STEPS = N_CHUNKS // STEP_CHUNKS
D_MAIN = D_FOURIER + D_SSD + CONV_DIM
D_IN_PROJ = D_MAIN + 2 * SSD_HEADS
DT_PAD = 128
D_FF = 4 * D_MODEL
EPS = 1e-6
LOG2_E = float(np.log2(np.e))

SUBLANES = 8
LANES = 128
COL_TILE = 512
FF_TILE = 1024
MLP_ROWS = 768
PROJ_ROWS = 1408
OUT_ROWS = 528
ADA_ROWS = 16
ROW_CHUNK = 128
PREP_CHUNKS = 6
FFT_N1 = 64
FFT_N2 = 128
VMEM_LIMIT = 62 * 1024 * 1024

N_PROJ_TILES = D_MAIN // COL_TILE
N_PROJ_ROW_BLOCKS = T_ALL // PROJ_ROWS
N_PROJ_STEPS = N_PROJ_ROW_BLOCKS * N_PROJ_TILES
FIRST_XBC_TILE = (D_FOURIER + D_SSD) // COL_TILE
Z_COL = 0
X_COL = D_SSD
BC_COL = 2 * D_SSD
F_COL = D_SSD + CONV_DIM


def _silu(v):
    return v * jax.nn.sigmoid(v)


def _is_ctx_rows(block_idx, rows_per_block):
    row = block_idx * rows_per_block + lax.broadcasted_iota(jnp.int32, (rows_per_block, 1), 0)
    return row >= SEQ


def _pick(is_ctx, ref):
    return jnp.where(is_ctx, ref[1:2, :], ref[0:1, :])


def _norm_modulate_rows(x_ref, ctx_ref, g_ref, sh_ref, sc_ref, xn_scr, block_idx, rows_per_block):
    g = g_ref[0]
    chunk = math.gcd(rows_per_block, ROW_CHUNK)
    assert SEQ % chunk == 0

    def body(c, carry):
        r0 = pl.multiple_of(c * chunk, chunk)
        row = block_idx * rows_per_block + r0
        is_ctx = row >= SEQ
        if ctx_ref is None:
            x = x_ref[pl.ds(r0, chunk), :]
        else:
            x = lax.cond(is_ctx,
                         lambda: ctx_ref[pl.ds(pl.multiple_of(row - SEQ, chunk), chunk), :],
                         lambda: x_ref[pl.ds(r0, chunk), :])
        y = x * lax.rsqrt(jnp.mean(x * x, axis=-1, keepdims=True) + EPS) * g
        y = y * (1.0 + _pick(is_ctx, sc_ref)) + _pick(is_ctx, sh_ref)
        xn_scr[pl.ds(r0, chunk), :] = y.astype(BF16)
        return carry

    lax.fori_loop(0, rows_per_block // chunk, body, 0)


def _ada_kernel(c_ref, w_ref, b_ref, o_ref):
    s = _silu(c_ref[...]).astype(BF16)
    o_ref[0] = jnp.dot(s, w_ref[0].astype(BF16), preferred_element_type=F32) + b_ref[0]


def _ada_call(cc, w_ada, b_ada):
    tn = 1024
    return pl.pallas_call(
        _ada_kernel,
        grid=(DEPTH, 6 * D_MODEL // tn),
        in_specs=[
            pl.BlockSpec((ADA_ROWS, D_MODEL), lambda l, j: (0, 0)),
            pl.BlockSpec((1, D_MODEL, tn), lambda l, j: (l, 0, j)),
            pl.BlockSpec((1, 1, tn), lambda l, j: (l, 0, j)),
        ],
        out_specs=pl.BlockSpec((1, ADA_ROWS, tn), lambda l, j: (l, 0, j)),
        out_shape=jax.ShapeDtypeStruct((DEPTH, ADA_ROWS, 6 * D_MODEL), F32),
        compiler_params=pltpu.CompilerParams(
            dimension_semantics=("arbitrary", "arbitrary"), vmem_limit_bytes=VMEM_LIMIT),
        name="ada_mod",
    )(cc, w_ada, b_ada.reshape(DEPTH, 1, 6 * D_MODEL))


def _conv_silu(v, w_ref, b_ref, group_rows, col_slice):
    rows, cols = v.shape
    tiles = group_rows // SUBLANES
    v4 = v.reshape(rows // group_rows, tiles, SUBLANES, cols)
    sub = lax.broadcasted_iota(jnp.int32, (1, 1, SUBLANES, cols), 2)
    zero_tile = jnp.zeros((rows // group_rows, 1, SUBLANES, cols), F32)
    half = D_CONV // 2
    w_ref = w_ref.at[:, :, col_slice]
    b_ref = b_ref.at[:, :, col_slice]
    acc = v4 * w_ref[0, half:half + 1, :]
    for k in range(D_CONV):
        off = k - half
        if off == 0:
            continue
        r = pltpu.roll(v4, (-off) % SUBLANES, axis=2)
        if off < 0:
            nbr = jnp.concatenate([zero_tile, r[:, :-1]], axis=1)
            shifted = jnp.where(sub < -off, nbr, r)
        else:
            nbr = jnp.concatenate([r[:, 1:], zero_tile], axis=1)
            shifted = jnp.where(sub < SUBLANES - off, r, nbr)
        acc = acc + shifted * w_ref[0, k:k + 1, :]
    return _silu(acc + b_ref[0]).reshape(rows, cols)


def _inproj_kernel(split, *refs):
    if split:
        (x_ref, ctx_ref, g_ref, sh_ref, sc_ref, w_ref, wdt_ref, cw_ref, cb_ref, wsrc1_ref, wsrc2_ref,
         p_ref, dt_ref, wdst1_ref, wdst2_ref, xn_scr) = refs
    else:
        (x_ref, g_ref, sh_ref, sc_ref, w_ref, wdt_ref, cw_ref, cb_ref, wsrc1_ref, wsrc2_ref,
         p_ref, dt_ref, wdst1_ref, wdst2_ref, xn_scr) = refs
        ctx_ref = None
    i = pl.program_id(0)
    j = pl.program_id(1)
    last = pl.num_programs(0) - 1
    wdst1_ref[...] = wsrc1_ref[0].astype(BF16)
    wdst2_ref[...] = wsrc2_ref[0].astype(BF16)

    @pl.when(j == 0)
    def _():
        _norm_modulate_rows(x_ref, ctx_ref, g_ref, sh_ref, sc_ref, xn_scr, i, PROJ_ROWS)
        lane = lax.broadcasted_iota(jnp.int32, (1, DT_PAD), 1)
        wdt = jnp.where(lane < 2 * SSD_HEADS, wdt_ref[0], jnp.zeros((), BF16))
        dt_ref[...] = jnp.dot(xn_scr[...], wdt, preferred_element_type=F32)

    def matmul():
        return jnp.dot(xn_scr[...], w_ref[0], preferred_element_type=F32)

    all_cols = slice(0, COL_TILE)

    @pl.when(j < FIRST_XBC_TILE)
    def _():
        p_ref[...] = matmul()

    @pl.when(jnp.logical_and(j >= FIRST_XBC_TILE, i < last))
    def _():
        bounds = (0, 384, 768, 1088, PROJ_ROWS)
        for r0, r1 in zip(bounds[:-1], bounds[1:]):
            acc = jnp.dot(xn_scr[r0:r1, :], w_ref[0], preferred_element_type=F32)
            p_ref[r0:r1, :] = _conv_silu(acc, cw_ref, cb_ref, GRID_W, all_cols)

    @pl.when(jnp.logical_and(j >= FIRST_XBC_TILE, i == last))
    def _():
        n_lat = PROJ_ROWS - CTX_LEN
        acc = matmul()
        p_ref[:n_lat, :] = _conv_silu(acc[:n_lat], cw_ref, cb_ref, GRID_W, all_cols)
        p_ref[n_lat:, :] = _conv_silu(acc[n_lat:], cw_ref, cb_ref, CTX_LEN, all_cols)


def _proj_col_block(j):
    return jnp.where(j == 0, N_PROJ_TILES - 1, j - 1)


def _stream_specs(stream, rows, n_grid_axes):
    first = (lambda i: (i, 0)) if n_grid_axes == 1 else (lambda i, j: (i, 0))
    zero = (lambda i: (0, 0)) if n_grid_axes == 1 else (lambda i, j: (0, 0))
    if isinstance(stream, tuple):
        return [pl.BlockSpec((rows, D_MODEL), first), pl.BlockSpec((CTX_LEN, D_MODEL), zero)], list(stream)
    return [pl.BlockSpec((rows, D_MODEL), first)], [stream]


def _inproj_call(l, xs, g_mix, mod, w_in, conv_w, conv_b, w_mlp1, w_mlp2):
    xbc_tile = lambda j: jnp.maximum(j - FIRST_XBC_TILE, 0)
    stream_specs, stream_args = _stream_specs(xs, PROJ_ROWS, 2)
    cast_steps = 32
    slab1 = D_MODEL // cast_steps
    slab2 = D_FF // cast_steps
    slab1_of = lambda i, j: jnp.minimum(i * N_PROJ_TILES + j, cast_steps - 1)
    slab2_of = lambda i, j: jnp.maximum(i * N_PROJ_TILES + j - (N_PROJ_STEPS - cast_steps), 0)
    return pl.pallas_call(
        functools.partial(_inproj_kernel, isinstance(xs, tuple)),
        grid=(N_PROJ_ROW_BLOCKS, N_PROJ_TILES),
        in_specs=stream_specs + [
            pl.BlockSpec((1, 1, D_MODEL), lambda i, j: (l, 0, 0)),
            pl.BlockSpec((8, D_MODEL), lambda i, j: (0, 0)),
            pl.BlockSpec((8, D_MODEL), lambda i, j: (0, 1)),
            pl.BlockSpec((1, D_MODEL, COL_TILE), lambda i, j: (l, 0, j)),
            pl.BlockSpec((1, D_MODEL, DT_PAD), lambda i, j: (l, 0, D_MAIN // DT_PAD)),
            pl.BlockSpec((1, D_CONV, COL_TILE), lambda i, j: (l, 0, xbc_tile(j))),
            pl.BlockSpec((1, 1, COL_TILE), lambda i, j: (l, 0, xbc_tile(j))),
            pl.BlockSpec((1, slab1, D_FF), lambda i, j: (l, slab1_of(i, j), 0)),
            pl.BlockSpec((1, slab2, D_MODEL), lambda i, j: (l, slab2_of(i, j), 0)),
        ],
        out_specs=[
            pl.BlockSpec((PROJ_ROWS, COL_TILE), lambda i, j: (i, _proj_col_block(j))),
            pl.BlockSpec((PROJ_ROWS, DT_PAD), lambda i, j: (i, 0)),
            pl.BlockSpec((slab1, D_FF), lambda i, j: (slab1_of(i, j), 0)),
            pl.BlockSpec((slab2, D_MODEL), lambda i, j: (slab2_of(i, j), 0)),
        ],
        out_shape=[
            jax.ShapeDtypeStruct((T_ALL, D_MAIN), F32),
            jax.ShapeDtypeStruct((T_ALL, DT_PAD), F32),
            jax.ShapeDtypeStruct((D_MODEL, D_FF), BF16),
            jax.ShapeDtypeStruct((D_FF, D_MODEL), BF16),
        ],
        scratch_shapes=[pltpu.VMEM((PROJ_ROWS, D_MODEL), BF16)],
        compiler_params=pltpu.CompilerParams(
            dimension_semantics=("arbitrary", "arbitrary"), vmem_limit_bytes=VMEM_LIMIT),
        name="in_proj",
    )(*stream_args, g_mix, mod, mod, w_in, w_in, conv_w, conv_b, w_mlp1, w_mlp2)


def _dft_tables():
    L, n1, n2 = SEQ, FFT_N1, FFT_N2
    a = np.arange(n1)[:, None, None]
    k2 = np.arange(n2)[None, :, None]
    b = np.arange(n2)[None, None, :]
    ang = 2.0 * np.pi * ((k2 * (a + n1 * b)) % L) / L
    t1 = np.concatenate([np.cos(ang), -np.sin(ang)], axis=1)
    k1 = np.arange(n1)[:, None]
    aa = np.arange(n1)[None, :]
    ang2 = 2.0 * np.pi * ((k1 * aa) % n1) / n1
    c2, s2 = np.cos(ang2), np.sin(ang2)
    f2 = np.block([[c2, s2], [-s2, c2]])
    cc = np.arange(FG_DIM)
    angc = 2.0 * np.pi * ((cc[:, None] * cc[None, :]) % FG_DIM) / FG_DIM
    lc = np.arange(CTX_LEN)
    angl = 2.0 * np.pi * ((lc[:, None] * lc[None, :]) % CTX_LEN) / CTX_LEN
    as32 = lambda v: jnp.asarray(v, dtype=F32)
    as16 = lambda v: jnp.asarray(v, dtype=F32).astype(BF16)
    return dict(t1=as16(t1), f2=as16(f2), cc=as32(np.cos(angc)), sc=as32(np.sin(angc)),
                cl=as32(np.cos(angl)), sl=as32(np.sin(angl)))


def _fourier_lat_kernel(u_ref, t1_ref, f2_ref, cc_ref, sc_ref, w_ref, o_ref,
                        zr_scr, zi_scr, xr_scr, xi_scr):
    n1, n2 = FFT_N1, FFT_N2
    scale = 1.0 / np.sqrt(float(SEQ) * FG_DIM)
    w = w_ref[0, 0]
    g1 = jnp.dot(cc_ref[...], w, precision=HIGHEST, preferred_element_type=F32) * scale
    g2 = jnp.dot(sc_ref[...], w, precision=HIGHEST, preferred_element_type=F32) * scale
    gmat = jnp.concatenate([g1, g2], axis=0).astype(BF16)

    batch = SUBLANES

    def stage1(a, carry):
        xa = u_ref[pl.ds(a, n2, stride=n1), :].astype(BF16)
        z = jnp.dot(t1_ref[a], xa, preferred_element_type=F32)
        zr_scr[a] = z[:n2]
        zi_scr[a] = z[n2:]
        return carry

    lax.fori_loop(0, n1, stage1, 0, unroll=8)

    def stage2(kb, carry):
        k2 = pl.multiple_of(kb * batch, batch)
        zr = jnp.swapaxes(zr_scr[:, pl.ds(k2, batch), :], 0, 1)
        zi = jnp.swapaxes(zi_scr[:, pl.ds(k2, batch), :], 0, 1)
        cols = [jnp.concatenate([zr[q], zi[q]], axis=0).astype(BF16) for q in range(batch)]
        rhs = jnp.concatenate(cols, axis=1)
        res = jnp.dot(f2_ref[...], rhs, preferred_element_type=F32)
        for q in range(batch):
            blk = res[:, q * FG_DIM:(q + 1) * FG_DIM]
            xr_scr[k2 + q] = blk[:n1]
            xi_scr[k2 + q] = blk[n1:]
        return carry

    lax.fori_loop(0, n2 // batch, stage2, 0, unroll=2)

    def finish(kb, carry):
        k1 = pl.multiple_of(kb * batch, batch)
        xr = jnp.swapaxes(xr_scr[:, pl.ds(k1, batch), :], 0, 1)
        xi = jnp.swapaxes(xi_scr[:, pl.ds(k1, batch), :], 0, 1)
        for q in range(batch):
            xri = jnp.concatenate([xr[q], xi[q]], axis=1).astype(BF16)
            o = jnp.dot(xri, gmat, preferred_element_type=F32)
            row0 = pl.multiple_of((k1 + q) * n2, n2)
            o_ref[pl.ds(row0, n2), :] = o.astype(o_ref.dtype)
        return carry

    lax.fori_loop(0, n1 // batch, finish, 0)


def _fourier_lat_call(l, p, w_f, tabs):
    col0 = F_COL // FG_DIM
    const = lambda shape: pl.BlockSpec(shape, lambda g: (0,) * len(shape))
    return pl.pallas_call(
        _fourier_lat_kernel,
        grid=(N_FGROUPS,),
        in_specs=[
            pl.BlockSpec((SEQ, FG_DIM), lambda g: (0, col0 + g)),
            const((FFT_N1, 2 * FFT_N2, FFT_N2)),
            const((2 * FFT_N1, 2 * FFT_N1)),
            const((FG_DIM, FG_DIM)),
            const((FG_DIM, FG_DIM)),
            pl.BlockSpec((1, 1, FG_DIM, FG_DIM), lambda g: (l, g, 0, 0)),
        ],
        out_specs=pl.BlockSpec((SEQ, FG_DIM), lambda g: (0, g)),
        out_shape=jax.ShapeDtypeStruct((T_ALL, D_FOURIER), BF16),
        scratch_shapes=([pltpu.VMEM((FFT_N1, FFT_N2, FG_DIM), F32)] * 2
                        + [pltpu.VMEM((FFT_N2, FFT_N1, FG_DIM), F32)] * 2),
        compiler_params=pltpu.CompilerParams(
            dimension_semantics=("arbitrary",), vmem_limit_bytes=VMEM_LIMIT),
        name="fourier_lat",
    )(p, tabs["t1"], tabs["f2"], tabs["cc"], tabs["sc"], w_f)


def _fourier_ctx_kernel(u_ref, cl_ref, sl_ref, cc_ref, sc_ref, w_ref, f_hbm_ref, o_ref):
    del f_hbm_ref
    scale = 1.0 / np.sqrt(float(CTX_LEN) * FG_DIM)
    dot = functools.partial(jnp.dot, precision=HIGHEST, preferred_element_type=F32)
    w = w_ref[0, 0]
    u = u_ref[...]
    a = dot(u, dot(cc_ref[...], w))
    b = dot(u, dot(sc_ref[...], w))
    o_ref[...] = ((dot(cl_ref[...], a) - dot(sl_ref[...], b)) * scale).astype(o_ref.dtype)


def _fourier_ctx_call(l, p, w_f, tabs, f):
    col0 = F_COL // FG_DIM
    row_blk = SEQ // CTX_LEN
    const = lambda shape: pl.BlockSpec(shape, lambda g: (0,) * len(shape))
    return pl.pallas_call(
        _fourier_ctx_kernel,
        grid=(N_FGROUPS,),
        in_specs=[
            pl.BlockSpec((CTX_LEN, FG_DIM), lambda g: (row_blk, col0 + g)),
            const((CTX_LEN, CTX_LEN)),
            const((CTX_LEN, CTX_LEN)),
            const((FG_DIM, FG_DIM)),
            const((FG_DIM, FG_DIM)),
            pl.BlockSpec((1, 1, FG_DIM, FG_DIM), lambda g: (l, g, 0, 0)),
            pl.BlockSpec(memory_space=pl.ANY),
        ],
        out_specs=pl.BlockSpec((CTX_LEN, FG_DIM), lambda g: (row_blk, g)),
        out_shape=jax.ShapeDtypeStruct((T_ALL, D_FOURIER), BF16),
        input_output_aliases={6: 0},
        compiler_params=pltpu.CompilerParams(
            dimension_semantics=("arbitrary",), vmem_limit_bytes=VMEM_LIMIT),
        name="fourier_ctx",
    )(p, tabs["cl"], tabs["sl"], tabs["cc"], tabs["sc"], w_f, f)


def _ssd_prep_kernel(dtr_ref, dtb_ref, alog_ref, tri_ref, cum_ref, tr_ref):
    q = CHUNK
    is_fwd = lax.broadcasted_iota(jnp.int32, (1, DT_PAD), 1) < SSD_HEADS
    neg_a = -jnp.exp(alog_ref[0])
    for c in range(PREP_CHUNKS):
        rs = slice(c * q, (c + 1) * q)
        v = dtr_ref[rs, :] + dtb_ref[0]
        dt = jnp.maximum(v, 0.0) + jnp.log1p(jnp.exp(-jnp.abs(v)))
        dta = dt * neg_a
        run = jnp.dot(tri_ref[0], dta, precision=HIGHEST, preferred_element_type=F32)
        rev = jnp.dot(tri_ref[1], dta, precision=HIGHEST, preferred_element_type=F32)
        cum = jnp.where(is_fwd, run, rev)
        total = jnp.where(is_fwd, run[q - 1:q, :], rev[0:1, :])
        cum2 = cum * LOG2_E
        cum_ref[rs, :] = cum2
        tr_ref[c, 0] = (cum2 - jnp.log2(dt)).T
        tr_ref[c, 1] = (dt * jnp.exp(total - cum)).T


def _ssd_prep_call(l, dtr, dtb, alog, tri):
    layer = lambda width: pl.BlockSpec((1, 1, width), lambda t: (l, 0, 0))
    rows = PREP_CHUNKS * CHUNK
    return pl.pallas_call(
        _ssd_prep_kernel,
        grid=(N_CHUNKS // PREP_CHUNKS,),
        in_specs=[pl.BlockSpec((rows, DT_PAD), lambda t: (t, 0)), layer(DT_PAD), layer(DT_PAD),
                  pl.BlockSpec((2, CHUNK, CHUNK), lambda t: (0, 0, 0))],
        out_specs=[pl.BlockSpec((rows, DT_PAD), lambda t: (t, 0)),
                   pl.BlockSpec((PREP_CHUNKS, 2, DT_PAD, CHUNK), lambda t: (t, 0, 0, 0))],
        out_shape=[jax.ShapeDtypeStruct((T_ALL, DT_PAD), F32),
                   jax.ShapeDtypeStruct((N_CHUNKS, 2, DT_PAD, CHUNK), F32)],
        compiler_params=pltpu.CompilerParams(
            dimension_semantics=("arbitrary",), vmem_limit_bytes=VMEM_LIMIT),
        name="ssd_prep",
    )(dtr, dtb, alog, tri)


def _ssd_kernel(direction, *refs):
    forward = direction == 0
    if forward:
        x_ref, bc_ref, cum_ref, tr_ref, o_ref, h_scr = refs
    else:
        x_ref, bc_ref, cum_ref, tr_ref, yf_ref, z_ref, dsk_ref, gn_ref, o_ref, h_scr = refs
    q = CHUNK
    edge = q - 1 if forward else 0

    @pl.when(pl.program_id(0) == 0)
    def _():
        h_scr[...] = jnp.zeros_like(h_scr)

    row = lax.broadcasted_iota(jnp.int32, (q, q), 0)
    col = lax.broadcasted_iota(jnp.int32, (q, q), 1)
    keep = (col <= row) if forward else (col >= row)
    low_lanes = col < SSD_HEAD_DIM

    for c in (range(STEP_CHUNKS) if forward else reversed(range(STEP_CHUNKS))):
        rs = slice(c * q, (c + 1) * q)
        cum2 = cum_ref[rs, :]
        cdt_t = tr_ref[c, 0]
        w_t = tr_ref[c, 1]
        x = x_ref[rs, :]
        x16 = x.astype(BF16)

        ys = []
        for g in range(SSD_GROUPS):
            bm = bc_ref[rs, g * D_STATE:(g + 1) * D_STATE]
            cm = bc_ref[rs, (SSD_GROUPS + g) * D_STATE:(SSD_GROUPS + g + 1) * D_STATE]
            bt = bm.T
            c16 = cm.astype(BF16)
            cb = jnp.dot(c16, bt.astype(BF16), preferred_element_type=F32)
            h_prev = [h_scr[PAIRS_PER_GROUP * g + j] for j in range(PAIRS_PER_GROUP)]
            y_off = jnp.dot(c16, jnp.concatenate(h_prev, axis=1).astype(BF16),
                            preferred_element_type=F32)
            for j in range(PAIRS_PER_GROUP):
                top, bot, ecol = [], [], []
                for r in range(2):
                    hh = SSD_HEADS * direction + HEADS_PER_GROUP * g + 2 * j + r
                    colb = jnp.broadcast_to(cum2[:, hh:hh + 1], (q, q))
                    seg = colb - cdt_t[hh:hh + 1, :]
                    decay_dt = jnp.exp2(jnp.where(keep, seg, -jnp.inf))
                    top.append((decay_dt * cb).astype(BF16))
                    bot.append((bt * w_t[hh:hh + 1, :]).astype(BF16))
                    ecol.append(jnp.exp2(colb))
                lhs = jnp.concatenate([jnp.concatenate(top, axis=1),
                                       jnp.concatenate(bot, axis=1)], axis=0)
                c0 = g * GROUP_DIM + j * LANES
                xp = x16[:, c0:c0 + LANES]
                zero = jnp.zeros_like(xp)
                rhs = jnp.concatenate([jnp.where(low_lanes, xp, zero),
                                       jnp.where(low_lanes, zero, xp)], axis=0)
                res = jnp.dot(lhs, rhs, preferred_element_type=F32)
                escale = jnp.where(low_lanes, ecol[0], ecol[1])
                ys.append(res[:q] + y_off[:, j * LANES:(j + 1) * LANES] * escale)
                h_scr[PAIRS_PER_GROUP * g + j] = h_prev[j] * escale[edge:edge + 1, :] + res[q:]
        y = jnp.concatenate(ys, axis=1)

        if forward:
            o_ref[rs, :] = y
        else:
            y = yf_ref[rs, :] + y + dsk_ref[0] * x
            u = y * _silu(z_ref[rs, :])
            outs = []
            for g in range(SSD_GROUPS):
                ug = u[:, g * GROUP_DIM:(g + 1) * GROUP_DIM]
                outs.append(ug * lax.rsqrt(jnp.mean(ug * ug, axis=-1, keepdims=True) + EPS))
            o_ref[rs, :] = (jnp.concatenate(outs, axis=1) * gn_ref[0]).astype(o_ref.dtype)


def _fwd_block(t):
    return jnp.where(t == 0, N_SSD_STEPS - 1, t - 1)


def _bwd_block(t):
    return N_SSD_STEPS - 1 - t


def _ssd_tri():
    lower = np.tril(np.ones((CHUNK, CHUNK), np.float32))
    return jnp.asarray(np.stack([lower, lower.T]))


def _ssd_call(l, direction, p, cum, tr, extra=None):
    block_of = _fwd_block if direction == 0 else _bwd_block
    rows = lambda width, cblk: pl.BlockSpec((STEP_CHUNKS * CHUNK, width), lambda t: (block_of(t), cblk))
    layer = lambda width: pl.BlockSpec((1, 1, width), lambda t: (l, 0, 0))
    bc_width = 2 * SSD_GROUPS * D_STATE
    in_specs = [rows(D_SSD, X_COL // D_SSD), rows(bc_width, BC_COL // bc_width), rows(DT_PAD, 0),
                pl.BlockSpec((STEP_CHUNKS, 2, DT_PAD, CHUNK), lambda t: (block_of(t), 0, 0, 0))]
    args = [p, p, cum, tr]
    if direction == 0:
        out_dtype = F32
    else:
        yf, dsk, gn = extra
        in_specs += [rows(D_SSD, 0), rows(D_SSD, Z_COL // D_SSD), layer(D_SSD), layer(D_SSD)]
        args += [yf, p, dsk, gn]
        out_dtype = BF16
    return pl.pallas_call(
        functools.partial(_ssd_kernel, direction),
        grid=(N_SSD_STEPS,),
        in_specs=in_specs,
        out_specs=rows(D_SSD, 0),
        out_shape=jax.ShapeDtypeStruct((T_ALL, D_SSD), out_dtype),
        scratch_shapes=[pltpu.VMEM((SSD_GROUPS * PAIRS_PER_GROUP, D_STATE, LANES), F32)],
        compiler_params=pltpu.CompilerParams(
            dimension_semantics=("arbitrary",), vmem_limit_bytes=VMEM_LIMIT),
        name="ssd_fwd" if direction == 0 else "ssd_bwd",
    )(*args)


def _outproj_kernel(split, *refs):
    if split:
        x_ref, ctx_ref, f_ref, y_ref, w_ref, gt_ref, o_ref, w16_scr = refs
    else:
        x_ref, f_ref, y_ref, w_ref, gt_ref, o_ref, w16_scr = refs
    i = pl.program_id(0)

    @pl.when(i == 0)
    def _():
        rows = 256

        def body(c, carry):
            rs = pl.ds(pl.multiple_of(c * rows, rows), rows)
            w16_scr[rs, :] = w_ref[0, rs, :].astype(BF16)
            return carry

        lax.fori_loop(0, D_MODEL // rows, body, 0)

    acc = (jnp.dot(f_ref[...], w16_scr[:D_FOURIER, :], preferred_element_type=F32)
           + jnp.dot(y_ref[...], w16_scr[D_FOURIER:, :], preferred_element_type=F32))
    if not split:
        gate = _pick(_is_ctx_rows(i, OUT_ROWS), gt_ref)
        o_ref[...] = x_ref[...] + gate * acc
    else:
        last = pl.num_programs(0) - 1
        n_lat = SEQ - (T_ALL // OUT_ROWS - 1) * OUT_ROWS

        @pl.when(i < last)
        def _():
            o_ref[...] = x_ref[...] + gt_ref[0:1, :] * acc

        @pl.when(i == last)
        def _():
            o_ref[:n_lat, :] = x_ref[:n_lat, :] + gt_ref[0:1, :] * acc[:n_lat]
            o_ref[n_lat:, :] = ctx_ref[...] + gt_ref[1:2, :] * acc[n_lat:]


def _outproj_call(l, xs, f, y, w_out, mod):
    rows = lambda width: pl.BlockSpec((OUT_ROWS, width), lambda i: (i, 0))
    stream_specs, stream_args = _stream_specs(xs, OUT_ROWS, 1)
    return pl.pallas_call(
        functools.partial(_outproj_kernel, isinstance(xs, tuple)),
        grid=(T_ALL // OUT_ROWS,),
        in_specs=stream_specs + [
            rows(D_FOURIER), rows(D_SSD),
            pl.BlockSpec((1, D_MODEL, D_MODEL), lambda i: (l, 0, 0), pipeline_mode=pl.Buffered(1)),
            pl.BlockSpec((8, D_MODEL), lambda i: (0, 2)),
        ],
        out_specs=rows(D_MODEL),
        out_shape=jax.ShapeDtypeStruct((T_ALL, D_MODEL), F32),
        scratch_shapes=[pltpu.VMEM((D_MODEL, D_MODEL), BF16)],
        compiler_params=pltpu.CompilerParams(
            dimension_semantics=("arbitrary",), vmem_limit_bytes=VMEM_LIMIT),
        name="out_proj",
    )(*stream_args, f, y, w_out, mod)


def _mlp_kernel(final, x_ref, g_ref, sh_ref, sc_ref, gt_ref, w1_ref, w2_ref, gf_ref, o_ref, xn_scr):
    i = pl.program_id(0)
    j = pl.program_id(1)

    @pl.when(j == 0)
    def _():
        _norm_modulate_rows(x_ref, None, g_ref, sh_ref, sc_ref, xn_scr, i, MLP_ROWS)
        o_ref[...] = jnp.zeros_like(o_ref)

    h = jnp.dot(xn_scr[...], w1_ref[...], preferred_element_type=F32)
    h = jnp.square(jnp.maximum(h, 0.0)).astype(BF16)
    o_ref[...] += jnp.dot(h, w2_ref[...], preferred_element_type=F32)

    @pl.when(j == pl.num_programs(1) - 1)
    def _():
        gate = _pick(_is_ctx_rows(i, MLP_ROWS), gt_ref)
        y = x_ref[...] + gate * o_ref[...]
        if final:
            y = y * lax.rsqrt(jnp.mean(y * y, axis=-1, keepdims=True) + EPS) * gf_ref[...]
        o_ref[...] = y


def _mlp_call(l, xs, g_mlp, mod, w1, w2, g_final, final):
    return pl.pallas_call(
        functools.partial(_mlp_kernel, final),
        grid=(T_ALL // MLP_ROWS, D_FF // FF_TILE),
        in_specs=[
            pl.BlockSpec((MLP_ROWS, D_MODEL), lambda i, j: (i, 0)),
            pl.BlockSpec((1, 1, D_MODEL), lambda i, j: (l, 0, 0)),
            pl.BlockSpec((8, D_MODEL), lambda i, j: (0, 3)),
            pl.BlockSpec((8, D_MODEL), lambda i, j: (0, 4)),
            pl.BlockSpec((8, D_MODEL), lambda i, j: (0, 5)),
            pl.BlockSpec((D_MODEL, FF_TILE), lambda i, j: (0, j)),
            pl.BlockSpec((FF_TILE, D_MODEL), lambda i, j: (j, 0)),
            pl.BlockSpec((1, D_MODEL), lambda i, j: (0, 0)),
        ],
        out_specs=pl.BlockSpec((MLP_ROWS, D_MODEL), lambda i, j: (i, 0)),
        out_shape=jax.ShapeDtypeStruct((SEQ if final else T_ALL, D_MODEL), F32),
        scratch_shapes=[pltpu.VMEM((MLP_ROWS, D_MODEL), BF16)],
        compiler_params=pltpu.CompilerParams(
            dimension_semantics=("arbitrary", "arbitrary"), vmem_limit_bytes=VMEM_LIMIT),
        name="mlp",
    )(xs, g_mlp, mod, mod, mod, w1, w2, g_final)


def _mixer_layer(l, xs, mod, tabs, tri, g_mix, w_in, conv_w, conv_b, dtb, alog, dsk, g_ssd_norm,
                 w_fourier, w_out, w_mlp1, w_mlp2):
    p, dtr, w1, w2 = _inproj_call(l, xs, g_mix, mod, w_in, conv_w, conv_b, w_mlp1, w_mlp2)
    f = _fourier_lat_call(l, p, w_fourier, tabs)
    f = _fourier_ctx_call(l, p, w_fourier, tabs, f)
    cum, tr = _ssd_prep_call(l, dtr, dtb, alog, tri)
    yf = _ssd_call(l, 0, p, cum, tr)
    y = _ssd_call(l, 1, p, cum, tr, extra=(yf, dsk, g_ssd_norm))
    return _outproj_call(l, xs, f, y, w_out, mod), w1, w2


def kernel(x, c, ctx, c_ctx, w_ada, b_ada, g_mix, w_in, conv_w, conv_b, dt_bias, a_log, d_skip,
           g_ssd_norm, w_fourier, w_out, g_mlp, w_mlp1, w_mlp2, g_final):
    assert x.shape == (1, SEQ, D_MODEL) and ctx.shape == (1, CTX_LEN, D_MODEL)
    tabs = _dft_tables()
    tri = _ssd_tri()

    xs = (x[0], ctx[0])
    cc = jnp.concatenate([c, c_ctx[None, :], jnp.zeros((ADA_ROWS - 2, D_MODEL), F32)], axis=0)
    mods = _ada_call(cc, w_ada, b_ada)

    pad_heads = ((0, 0), (0, 0), (0, DT_PAD - 2 * SSD_HEADS))
    dtb = jnp.pad(dt_bias.reshape(DEPTH, 1, 2 * SSD_HEADS), pad_heads)
    alog = jnp.pad(a_log.reshape(DEPTH, 1, 2 * SSD_HEADS), pad_heads)
    dsk = jnp.repeat(d_skip, SSD_HEAD_DIM, axis=1).reshape(DEPTH, 1, D_SSD)
    row3 = lambda a: a.reshape(DEPTH, 1, a.shape[-1])
    w_in16 = w_in.astype(BF16)

    for l in range(DEPTH):
        xs, w1, w2 = _mixer_layer(l, xs, mods[l], tabs, tri, row3(g_mix), w_in16, conv_w, row3(conv_b),
                                  dtb, alog, dsk, row3(g_ssd_norm), w_fourier, w_out, w_mlp1, w_mlp2)
        xs = _mlp_call(l, xs, row3(g_mlp), mods[l], w1, w2, g_final[None, :], final=l == DEPTH - 1)

    return xs[None]
```
